```python
import jax, jax.numpy as jnp
from jax import lax
import numpy as np

D_MODEL = 2048
BATCH = 2
SEQ = 4096
DEPTH = 4
DEC_BATCH = 8
DEC_SEQ = 4
PAST_LEN = 16384
PAGE_SIZE = 128

N_MIXERS = 2
N_POOL_LAYERS = (DEPTH + 1) // 2
N_FOX_LAYERS = DEPTH // 2
N_HEADS = 16
HEAD_DIM = D_MODEL // N_HEADS
POOL_WINDOWS = (2, 4, 8, 16)
N_POOL_GROUPS = len(POOL_WINDOWS)
POOL_GROUP_DIM = D_MODEL // N_POOL_GROUPS
POOL_CTX = max(POOL_WINDOWS) - 1
D_FF = ((8 * D_MODEL // 3 + 127) // 128) * 128
CONV_WIDTH = 3
CONV_CTX = CONV_WIDTH - 1
PLE_DIM = 256
BLOCK_Q = 128
RMS_EPS = 1e-6
FORGET_BIAS_INIT = 6.0
FORGET_W_SCALE = 0.1

kernel_name = 'hybrid_pool_fox_convffn_step'


def rmsnorm(x, g):
    xf = x.astype(jnp.float32)
    y = xf * lax.rsqrt(jnp.mean(xf * xf, axis=-1, keepdims=True) + RMS_EPS)
    return (y * g.astype(jnp.float32)).astype(x.dtype)


def pool_mix(ext, n_prev, pos0, w_pool, scale):
    b, l, _ = ext.shape
    t = l - n_prev
    xg = ext.astype(jnp.float32).reshape(b, l, N_POOL_GROUPS, POOL_GROUP_DIM)
    csum = jnp.concatenate([jnp.zeros_like(xg[:, :1]), jnp.cumsum(xg, axis=1)], axis=1)
    rows = jnp.arange(n_prev, l)
    pos = pos0 + jnp.arange(t)
    means = []
    for g, w in enumerate(POOL_WINDOWS):
        cg = csum[:, :, g]
        lo = jnp.maximum(rows + 1 - w, 0)
        cnt = jnp.minimum(w, pos + 1).astype(jnp.float32)
        means.append((cg[:, rows + 1] - cg[:, lo]) / cnt[None, :, None])
    pooled = jnp.stack(means, axis=2)
    delta = (pooled - xg[:, n_prev:]).astype(ext.dtype)
    y = jnp.einsum('btgc,gcd->btgd', delta, w_pool).reshape(b, t, D_MODEL)
    return y * scale


def fox_project(xn, w_qkvf, b_f):
    b, t, _ = xn.shape
    proj = xn @ w_qkvf
    q = proj[..., :D_MODEL].reshape(b, t, N_HEADS, HEAD_DIM)
    k = proj[..., D_MODEL:2 * D_MODEL].reshape(b, t, N_HEADS, HEAD_DIM)
    v = proj[..., 2 * D_MODEL:3 * D_MODEL].reshape(b, t, N_HEADS, HEAD_DIM)
    lf = jax.nn.log_sigmoid(proj[..., 3 * D_MODEL:].astype(jnp.float32) + b_f.astype(jnp.float32))
    return q, k, v, lf


def fox_prompt(q, k, v, lf):
    b, s = q.shape[:2]
    scale = HEAD_DIM ** -0.5
    c = jnp.cumsum(lf, axis=1).transpose(0, 2, 1)
    kpos = jnp.arange(s)

    def block(start):
        qb = lax.dynamic_slice_in_dim(q, start, BLOCK_Q, axis=1)
        cq = lax.dynamic_slice_in_dim(c, start, BLOCK_Q, axis=2)
        logits = (jnp.einsum('bqhd,bkhd->bhqk', qb, k).astype(jnp.float32) * scale
                  + cq[..., :, None] - c[:, :, None, :])
        qpos = start + jnp.arange(BLOCK_Q)
        logits = jnp.where(kpos[None, :] <= qpos[:, None], logits, -jnp.inf)
        p = jax.nn.softmax(logits, axis=-1).astype(v.dtype)
        return jnp.einsum('bhqk,bkhd->bqhd', p, v)

    out = lax.map(block, jnp.arange(s // BLOCK_Q) * BLOCK_Q)
    return out.transpose(1, 0, 2, 3, 4).reshape(b, s, D_MODEL)


def fox_sample(q, k, v, lf, k_past, v_past, lf_past):
    b, t = q.shape[:2]
    p_len = k_past.shape[1]
    scale = HEAD_DIM ** -0.5
    lfp = lf_past.astype(jnp.float32)
    suffix = (lax.cumsum(lfp, axis=1, reverse=True) - lfp).transpose(0, 2, 1)
    cn = jnp.cumsum(lf, axis=1).transpose(0, 2, 1)
    s_past = (jnp.einsum('bqhd,bkhd->bhqk', q, k_past).astype(jnp.float32) * scale
              + cn[..., :, None] + suffix[:, :, None, :])
    s_new = (jnp.einsum('bqhd,bkhd->bhqk', q, k).astype(jnp.float32) * scale
             + cn[..., :, None] - cn[:, :, None, :])
    s_new = jnp.where(jnp.tril(jnp.ones((t, t), dtype=bool)), s_new, -jnp.inf)
    p = jax.nn.softmax(jnp.concatenate([s_past, s_new], axis=-1), axis=-1).astype(v.dtype)
    out = (jnp.einsum('bhqk,bkhd->bqhd', p[..., :p_len], v_past)
           + jnp.einsum('bhqk,bkhd->bqhd', p[..., p_len:], v))
    return out.reshape(b, t, D_MODEL)


def conv_ffn(hn, prev_u, w_up, conv_w, conv_b, w_down):
    t = hn.shape[1]
    up = hn @ w_up
    u, v = up[..., :D_FF], up[..., D_FF:]
    u_ext = jnp.concatenate([prev_u.astype(u.dtype), u], axis=1)
    c = conv_b
    for tap in range(CONV_WIDTH):
        c = c + conv_w[tap] * u_ext[:, tap:tap + t]
    y = (jax.nn.gelu(c) * v) @ w_down
    return y, u_ext[:, -CONV_CTX:]


def trunk(x, p, pos0, pool_ctx, conv_ctx, fox_cache, page_table,
          g_mix, w_pool, pool_scale, w_qkvf, b_f, w_o,
          g_ffn, w_up, conv_w, conv_b, w_down, w_pe, g_ple, w_pg, g_final):
    h = x
    new_pool, new_conv, new_k, new_v, new_lf = [], [], [], [], []
    for i in range(DEPTH):
        xn = rmsnorm(h, g_mix[i])
        j = i // N_MIXERS
        if i % N_MIXERS == 0:
            ext = jnp.concatenate([pool_ctx[j].astype(xn.dtype), xn], axis=1)
            mix = pool_mix(ext, pool_ctx.shape[2], pos0, w_pool[j], pool_scale[j])
            new_pool.append(ext[:, -POOL_CTX:])
        else:
            q, k, v, lf = fox_project(xn, w_qkvf[j], b_f[j])
            if fox_cache is None:
                o = fox_prompt(q, k, v, lf)
            else:
                cache_k, cache_v, cache_lf = fox_cache
                nb, n_pages = page_table.shape
                k_past = cache_k[j][page_table].reshape(nb, n_pages * PAGE_SIZE, N_HEADS, HEAD_DIM)
                v_past = cache_v[j][page_table].reshape(nb, n_pages * PAGE_SIZE, N_HEADS, HEAD_DIM)
                lf_past = cache_lf[j][page_table].reshape(nb, n_pages * PAGE_SIZE, N_HEADS)
                o = fox_sample(q, k, v, lf, k_past, v_past, lf_past)
            mix = o @ w_o[j]
            new_k.append(k)
            new_v.append(v)
            new_lf.append(lf)
        h = h + mix
        f, cstate = conv_ffn(rmsnorm(h, g_ffn[i]), conv_ctx[i], w_up[i], conv_w[i], conv_b[i], w_down[i])
        new_conv.append(cstate)
        h = h + f
        h = h + (p[i] @ w_pe[i]) * jax.nn.sigmoid(rmsnorm(h, g_ple[i]) @ w_pg[i])
    y = rmsnorm(h, g_final)
    return (y, jnp.stack(new_pool), jnp.stack(new_conv),
            jnp.stack(new_k), jnp.stack(new_v), jnp.stack(new_lf))


def setup_inputs(seed: int = 0) -> dict:
    key = jax.random.key(seed)
    ks = jax.random.split(key, 32)
    n_pages = PAST_LEN // PAGE_SIZE
    n_phys = (DEC_BATCH * n_pages * 5) // 4
    nrm = jax.random.normal
    f32 = jnp.float32
    page_table = jax.random.permutation(ks[0], n_phys)[:DEC_BATCH * n_pages].reshape(DEC_BATCH, n_pages).astype(jnp.int32)
    w_qkv = nrm(ks[13], (N_FOX_LAYERS, D_MODEL, 3 * D_MODEL), f32) * D_MODEL ** -0.5
    w_fg = nrm(ks[25], (N_FOX_LAYERS, D_MODEL, N_HEADS), f32) * (FORGET_W_SCALE * D_MODEL ** -0.5)
    return {
        'x_prompt': nrm(ks[1], (BATCH, SEQ, D_MODEL), f32),
        'x_sample': nrm(ks[2], (DEC_BATCH, DEC_SEQ, D_MODEL), f32),
        'state_pool': nrm(ks[3], (N_POOL_LAYERS, DEC_BATCH, POOL_CTX, D_MODEL), f32),
        'state_conv': nrm(ks[4], (DEPTH, DEC_BATCH, CONV_CTX, D_FF), f32),
        'cache_k': nrm(ks[5], (N_FOX_LAYERS, n_phys, PAGE_SIZE, N_HEADS, HEAD_DIM), f32),
        'cache_v': nrm(ks[6], (N_FOX_LAYERS, n_phys, PAGE_SIZE, N_HEADS, HEAD_DIM), f32),
        'cache_lf': jax.nn.log_sigmoid(FORGET_BIAS_INIT + 0.1 * nrm(ks[7], (N_FOX_LAYERS, n_phys, PAGE_SIZE, N_HEADS), f32)),
        'page_table': page_table,
        'p_prompt': nrm(ks[8], (DEPTH, BATCH, SEQ, PLE_DIM), f32),
        'p_sample': nrm(ks[9], (DEPTH, DEC_BATCH, DEC_SEQ, PLE_DIM), f32),
        'g_mix': 1.0 + 0.1 * nrm(ks[10], (DEPTH, D_MODEL), f32),
        'w_pool': nrm(ks[11], (N_POOL_LAYERS, N_POOL_GROUPS, POOL_GROUP_DIM, POOL_GROUP_DIM), f32) * POOL_GROUP_DIM ** -0.5,
        'pool_scale': 1.0 + 0.1 * nrm(ks[12], (N_POOL_LAYERS, D_MODEL), f32),
        'w_qkvf': jnp.concatenate([w_qkv, w_fg], axis=-1),
        'b_f': FORGET_BIAS_INIT + 0.1 * nrm(ks[14], (N_FOX_LAYERS, N_HEADS), f32),
        'w_o': nrm(ks[15], (N_FOX_LAYERS, D_MODEL, D_MODEL), f32) * D_MODEL ** -0.5,
        'g_ffn': 1.0 + 0.1 * nrm(ks[16], (DEPTH, D_MODEL), f32),
        'w_up': nrm(ks[17], (DEPTH, D_MODEL, 2 * D_FF), f32) * D_MODEL ** -0.5,
        'conv_w': nrm(ks[18], (DEPTH, CONV_WIDTH, D_FF), f32) * CONV_WIDTH ** -0.5,
        'conv_b': 0.02 * nrm(ks[19], (DEPTH, D_FF), f32),
        'w_down': nrm(ks[20], (DEPTH, D_FF, D_MODEL), f32) * D_FF ** -0.5,
        'w_pe': nrm(ks[21], (DEPTH, PLE_DIM, D_MODEL), f32) * PLE_DIM ** -0.5,
        'g_ple': 1.0 + 0.1 * nrm(ks[22], (DEPTH, D_MODEL), f32),
        'w_pg': nrm(ks[23], (DEPTH, D_MODEL, D_MODEL), f32) * D_MODEL ** -0.5,
        'g_final': 1.0 + 0.1 * nrm(ks[24], (D_MODEL,), f32),
    }


def reference(x_prompt, x_sample, state_pool, state_conv, cache_k, cache_v, cache_lf, page_table,
              p_prompt, p_sample, g_mix, w_pool, pool_scale, w_qkvf, b_f, w_o,
              g_ffn, w_up, conv_w, conv_b, w_down, w_pe, g_ple, w_pg, g_final):
    b = x_prompt.shape[0]
    pool_ctx_prompt = jnp.zeros((N_POOL_LAYERS, b, 0, D_MODEL), x_prompt.dtype)
    conv_ctx_prompt = jnp.zeros((DEPTH, b, CONV_CTX, D_FF), x_prompt.dtype)
    y_prompt, pool_p, conv_p, k_p, v_p, lf_p = trunk(
        x_prompt, p_prompt, 0, pool_ctx_prompt, conv_ctx_prompt, None, None,
        g_mix, w_pool, pool_scale, w_qkvf, b_f, w_o,
        g_ffn, w_up, conv_w, conv_b, w_down, w_pe, g_ple, w_pg, g_final)
    past_len = page_table.shape[1] * PAGE_SIZE
    y_sample, pool_s, conv_s, k_s, v_s, lf_s = trunk(
        x_sample, p_sample, past_len, state_pool, state_conv, (cache_k, cache_v, cache_lf), page_table,
        g_mix, w_pool, pool_scale, w_qkvf, b_f, w_o,
        g_ffn, w_up, conv_w, conv_b, w_down, w_pe, g_ple, w_pg, g_final)
    return (y_prompt, y_sample, pool_p, pool_s, conv_p, conv_s, k_p, v_p, lf_p, k_s, v_s, lf_s)
```

```python
import functools

import jax
import jax.numpy as jnp
from jax import lax
from jax.experimental import pallas as pl
from jax.experimental.pallas import tpu as pltpu

D_MODEL = 2048
N_HEADS = 16
HEAD_DIM = D_MODEL // N_HEADS
POOL_WINDOWS = (2, 4, 8, 16)
POOL_GROUP_DIM = D_MODEL // len(POOL_WINDOWS)
POOL_SPAN = max(POOL_WINDOWS)
CONV_WIDTH = 3
CONV_CTX = CONV_WIDTH - 1
PAGE_SIZE = 128
RMS_EPS = 1e-6
ATTN_SCALE = HEAD_DIM ** -0.5

LANES = 128
SUBLANES = 8
FF_TILE = 512
VMEM_LIMIT = 56 * 1024 * 1024

F32 = jnp.float32
BF16 = jnp.bfloat16
NEG_INF = float("-inf")


def _params(n_axes):
    return pltpu.CompilerParams(dimension_semantics=("arbitrary",) * n_axes,
                                vmem_limit_bytes=VMEM_LIMIT)


def _rms(x, g):
    return x * lax.rsqrt(jnp.mean(x * x, axis=-1, keepdims=True) + RMS_EPS) * g


def _dot(a, b):
    return jnp.dot(a, b, preferred_element_type=F32)


def _dot_nt(a, b):
    return lax.dot_general(a, b, (((1,), (1,)), ((), ())), preferred_element_type=F32)


def _pool_kernel(h_ref, g_ref, prev_ref, w_ref, sc_ref, o_ref, tail_ref, ext_s,
                 *, tm, shift, tiles_per_seq, pos0):
    i = pl.program_id(0)
    span = POOL_SPAN * shift
    it = i % tiles_per_seq

    @pl.when(it == 0)
    def _():
        ext_s[0:span, :] = prev_ref[...]

    x = h_ref[...]
    xn = _rms(x, g_ref[...])
    ext_s[span:span + tm, :] = xn
    row = lax.broadcasted_iota(jnp.int32, (tm, 1), 0)
    pos = pos0 + (it * tm + row) // shift
    for g, w in enumerate(POOL_WINDOWS):
        cols = slice(g * POOL_GROUP_DIM, (g + 1) * POOL_GROUP_DIM)
        acc = xn[:, cols]
        for k in range(1, w):
            acc = acc + ext_s[span - k * shift:span - k * shift + tm, cols]
        cnt = jnp.minimum(w, pos + 1).astype(F32)
        delta = acc / cnt - xn[:, cols]
        y = _dot(delta.astype(BF16), w_ref[g])
        o_ref[:, cols] = x[:, cols] + y * sc_ref[:, cols]
    tail = ext_s[tm:tm + span, :]
    tail_ref[...] = tail
    if tiles_per_seq > 1:
        ext_s[0:span, :] = tail


def _pool_layer(h, g, prev0, w_pool, scale, *, tm, shift, seq_rows, pos0):
    m = h.shape[0]
    span = POOL_SPAN * shift
    tps = seq_rows // tm
    nseq = m // seq_rows
    assert tps == 1 or tm >= span
    kern = functools.partial(_pool_kernel, tm=tm, shift=shift, tiles_per_seq=tps, pos0=pos0)
    return pl.pallas_call(
        kern,
        grid=(m // tm,),
        in_specs=[
            pl.BlockSpec((tm, D_MODEL), lambda i: (i, 0)),
            pl.BlockSpec((1, D_MODEL), lambda i: (0, 0)),
            pl.BlockSpec((None, span, D_MODEL), lambda i: (i // tps, 0, 0)),
            pl.BlockSpec((len(POOL_WINDOWS), POOL_GROUP_DIM, POOL_GROUP_DIM), lambda i: (0, 0, 0)),
            pl.BlockSpec((1, D_MODEL), lambda i: (0, 0)),
        ],
        out_specs=[
            pl.BlockSpec((tm, D_MODEL), lambda i: (i, 0)),
            pl.BlockSpec((None, span, D_MODEL), lambda i: (i // tps, 0, 0)),
        ],
        out_shape=[jax.ShapeDtypeStruct((m, D_MODEL), F32),
                   jax.ShapeDtypeStruct((nseq, span, D_MODEL), F32)],
        scratch_shapes=[pltpu.VMEM((span + tm, D_MODEL), F32)],
        compiler_params=_params(1),
        name="pool_layer",
    )(h, g, prev0, w_pool, scale)


QKV_TILE = 512
HEADS_PER_TILE = QKV_TILE // HEAD_DIM
TILES_PER_PROJ = D_MODEL // QKV_TILE


def _qkvf_kernel(h_ref, g_ref, w_ref, wf_ref, bf_ref,
                 qh_ref, kh_ref, vh_ref, k_ref, v_ref, lf_ref, xn_s):
    j = pl.program_id(1)

    @pl.when(j == 0)
    def _():
        xn = _rms(h_ref[...], g_ref[...]).astype(BF16)
        xn_s[...] = xn
        z = _dot(xn, wf_ref[...]) + bf_ref[...]
        lf_ref[...] = jnp.minimum(z, 0.0) - jnp.log1p(jnp.exp(-jnp.abs(z)))

    res = _dot(xn_s[...], w_ref[...])

    def heads(dst):
        for hh in range(HEADS_PER_TILE):
            dst[hh] = res[:, hh * HEAD_DIM:(hh + 1) * HEAD_DIM].astype(BF16)

    @pl.when(j < TILES_PER_PROJ)
    def _():
        heads(qh_ref)

    @pl.when((j >= TILES_PER_PROJ) & (j < 2 * TILES_PER_PROJ))
    def _():
        heads(kh_ref)
        k_ref[...] = res

    @pl.when(j >= 2 * TILES_PER_PROJ)
    def _():
        heads(vh_ref)
        v_ref[...] = res


def _qkvf(h, g, w_qkv, w_f, b_f, *, tm):
    m = h.shape[0]
    tpp = TILES_PER_PROJ

    def sel(which):
        return lambda i, j: jnp.clip(j - which * tpp, 0, tpp - 1)

    hm_shape = jax.ShapeDtypeStruct((N_HEADS, m, HEAD_DIM), BF16)
    flat_shape = jax.ShapeDtypeStruct((m, D_MODEL), F32)

    def hm_spec(which):
        s = sel(which)
        return pl.BlockSpec((HEADS_PER_TILE, tm, HEAD_DIM), lambda i, j: (s(i, j), i, 0))

    def flat_spec(which):
        s = sel(which)
        return pl.BlockSpec((tm, QKV_TILE), lambda i, j: (i, s(i, j)))

    return pl.pallas_call(
        _qkvf_kernel,
        grid=(m // tm, 3 * tpp),
        in_specs=[
            pl.BlockSpec((tm, D_MODEL), lambda i, j: (i, 0)),
            pl.BlockSpec((1, D_MODEL), lambda i, j: (0, 0)),
            pl.BlockSpec((D_MODEL, QKV_TILE), lambda i, j: (0, j)),
            pl.BlockSpec((D_MODEL, N_HEADS), lambda i, j: (0, 0)),
            pl.BlockSpec((1, N_HEADS), lambda i, j: (0, 0)),
        ],
        out_specs=[hm_spec(0), hm_spec(1), hm_spec(2), flat_spec(1), flat_spec(2),
                   pl.BlockSpec((tm, N_HEADS), lambda i, j: (i, 0))],
        out_shape=[hm_shape, hm_shape, hm_shape, flat_shape, flat_shape,
                   jax.ShapeDtypeStruct((m, N_HEADS), F32)],
        scratch_shapes=[pltpu.VMEM((tm, D_MODEL), BF16)],
        compiler_params=_params(2),
        name="fox_qkvf",
    )(h, g, w_qkv, w_f, b_f)


def _lane_cumsum(x):
    n = x.shape[-1]
    lane = lax.broadcasted_iota(jnp.int32, x.shape, x.ndim - 1)
    k = 1
    while k < n:
        x = x + jnp.where(lane >= k, pltpu.roll(x, k, x.ndim - 1), 0.0)
        k *= 2
    return x


def _cumsum_kernel(x_ref, o_ref):
    o_ref[...] = _lane_cumsum(x_ref[...])


def _cumsum_last(x):
    b, hh, s = x.shape
    return pl.pallas_call(
        _cumsum_kernel,
        grid=(b,),
        in_specs=[pl.BlockSpec((None, hh, s), lambda i: (i, 0, 0))],
        out_specs=pl.BlockSpec((None, hh, s), lambda i: (i, 0, 0)),
        out_shape=jax.ShapeDtypeStruct(x.shape, F32),
        compiler_params=_params(1),
        name="forget_cumsum",
    )(x)


def _flash_kernel(q_ref, k_ref, v_ref, c_ref, ct_ref, o_ref, m_s, l_s, acc_s, cq_s, *, tq):
    qi = pl.program_id(1)
    ki = pl.program_id(2)

    @pl.when(ki == 0)
    def _():
        m_s[...] = jnp.full(m_s.shape, NEG_INF, F32)
        l_s[...] = jnp.zeros(l_s.shape, F32)
        acc_s[...] = jnp.zeros(acc_s.shape, F32)
        c = c_ref[...]
        for h in range(N_HEADS):
            cq_s[h] = c[:, h:h + 1]

    def step(diagonal):
        def body(h, carry):
            s = _dot_nt(q_ref[h], k_ref[h]) * ATTN_SCALE
            s = s + cq_s[h] - ct_ref[pl.ds(h, 1), :]
            if diagonal:
                row = lax.broadcasted_iota(jnp.int32, (tq, tq), 0)
                col = lax.broadcasted_iota(jnp.int32, (tq, tq), 1)
                s = jnp.where(col <= row, s, NEG_INF)
            m_prev = m_s[h]
            m_new = jnp.maximum(m_prev, jnp.max(s, axis=-1, keepdims=True))
            alpha = jnp.exp(m_prev - m_new)
            p = jnp.exp(s - m_new)
            l_s[h] = alpha * l_s[h] + jnp.sum(p, axis=-1, keepdims=True)
            acc_s[h] = alpha * acc_s[h] + _dot(p.astype(BF16), v_ref[h])
            m_s[h] = m_new
            return carry

        lax.fori_loop(0, N_HEADS, body, 0)

    @pl.when(ki < qi)
    def _():
        step(False)

    @pl.when(ki == qi)
    def _():
        step(True)

        def fin(h, carry):
            o_ref[h] = (acc_s[h] / l_s[h]).astype(BF16)
            return carry

        lax.fori_loop(0, N_HEADS, fin, 0)


def _flash_prompt(qh, kh, vh, c, ct, *, batch, seq, tq):
    nq = seq // tq

    def q_map(b, qi, ki):
        return (0, b * nq + qi, 0)

    def kv_map(b, qi, ki):
        return (0, b * nq + jnp.minimum(ki, qi), 0)

    kern = functools.partial(_flash_kernel, tq=tq)
    col = pltpu.VMEM((N_HEADS, tq, 1), F32)
    return pl.pallas_call(
        kern,
        grid=(batch, nq, nq),
        in_specs=[
            pl.BlockSpec((N_HEADS, tq, HEAD_DIM), q_map),
            pl.BlockSpec((N_HEADS, tq, HEAD_DIM), kv_map),
            pl.BlockSpec((N_HEADS, tq, HEAD_DIM), kv_map),
            pl.BlockSpec((None, tq, N_HEADS), lambda b, qi, ki: (b, qi, 0)),
            pl.BlockSpec((None, N_HEADS, tq), lambda b, qi, ki: (b, 0, jnp.minimum(ki, qi))),
        ],
        out_specs=pl.BlockSpec((N_HEADS, tq, HEAD_DIM), q_map),
        out_shape=jax.ShapeDtypeStruct(qh.shape, BF16),
        scratch_shapes=[col, col, pltpu.VMEM((N_HEADS, tq, HEAD_DIM), F32), col],
        compiler_params=_params(3),
        name="fox_prompt_attention",
    )(qh, kh, vh, c, ct)


DEC_T = 4
DEC_ROWS = DEC_T * N_HEADS


def _row_to_col(r):
    n = r.shape[-1]
    eye = (lax.broadcasted_iota(jnp.int32, (n, n), 0) == lax.broadcasted_iota(jnp.int32, (n, n), 1))
    return jnp.sum(jnp.where(eye, jnp.broadcast_to(r, (n, n)), 0.0), axis=1, keepdims=True)


def _decode_kernel(pt_ref, q_ref, kn_ref, vn_ref, lfn_ref, k_ref, v_ref, lfp_ref, o_ref,
                   m_s, l_s, acc_s, carry_s, qb_s, cn_s):
    pg = pl.program_id(1)
    head_of_lane = lax.broadcasted_iota(jnp.int32, (N_HEADS, D_MODEL), 1) // HEAD_DIM
    head_mask = (head_of_lane == lax.broadcasted_iota(jnp.int32, (N_HEADS, D_MODEL), 0)).astype(F32)

    @pl.when(pg == 0)
    def _():
        q = q_ref[...]
        qb = jnp.concatenate(
            [jnp.broadcast_to(q[t:t + 1, :], (N_HEADS, D_MODEL)) * head_mask for t in range(DEC_T)],
            axis=0).astype(BF16)
        qb_s[...] = qb
        lfn = lfn_ref[...]
        run = jnp.zeros((1, N_HEADS), F32)
        cn_cols = []
        for t in range(DEC_T):
            run = run + lfn[t:t + 1, :]
            cn_cols.append(_row_to_col(run))
        cn_s[...] = jnp.concatenate(cn_cols, axis=0)
        fill = jnp.zeros((PAGE_SIZE - SUBLANES, D_MODEL), BF16)
        kn = jnp.concatenate([kn_ref[...].astype(BF16), fill], axis=0)
        vn = jnp.concatenate([vn_ref[...].astype(BF16), fill], axis=0)
        s_new = _dot_nt(qb, kn) * ATTN_SCALE
        lane = lax.broadcasted_iota(jnp.int32, (1, PAGE_SIZE), 1)
        cn_mat = jnp.zeros((N_HEADS, PAGE_SIZE), F32)
        for t in range(DEC_T):
            cn_mat = cn_mat + cn_cols[t] * (lane == t).astype(F32)
        blocks = []
        for t in range(DEC_T):
            blk = s_new[t * N_HEADS:(t + 1) * N_HEADS, :] + (cn_cols[t] - cn_mat)
            blocks.append(jnp.where(lane <= t, blk, NEG_INF))
        s_new = jnp.concatenate(blocks, axis=0)
        m0 = jnp.max(s_new, axis=-1, keepdims=True)
        p0 = jnp.exp(s_new - m0)
        m_s[...] = m0
        l_s[...] = jnp.sum(p0, axis=-1, keepdims=True)
        acc_s[...] = _dot(p0.astype(BF16), vn)
        carry_s[...] = jnp.zeros(carry_s.shape, F32)

    s = _dot_nt(qb_s[...], k_ref[...].astype(BF16)) * ATTN_SCALE
    lfp = lfp_ref[...]
    lf_t = jnp.concatenate([lfp, jnp.zeros((PAGE_SIZE, LANES - N_HEADS), F32)], axis=1).T[:N_HEADS]
    pre = _lane_cumsum(lf_t)
    tot = pre[:, PAGE_SIZE - 1:PAGE_SIZE]
    suffix = (tot - pre) + carry_s[...]
    s = s + jnp.concatenate([suffix] * DEC_T, axis=0) + cn_s[...]
    m_prev = m_s[...]
    m_new = jnp.maximum(m_prev, jnp.max(s, axis=-1, keepdims=True))
    alpha = jnp.exp(m_prev - m_new)
    p = jnp.exp(s - m_new)
    l_s[...] = alpha * l_s[...] + jnp.sum(p, axis=-1, keepdims=True)
    acc_s[...] = alpha * acc_s[...] + _dot(p.astype(BF16), v_ref[...].astype(BF16))
    m_s[...] = m_new
    carry_s[...] = carry_s[...] + tot

    @pl.when(pg == pl.num_programs(1) - 1)
    def _():
        out = acc_s[...] / l_s[...]
        rows = [jnp.sum(out[t * N_HEADS:(t + 1) * N_HEADS, :] * head_mask, axis=0, keepdims=True)
                for t in range(DEC_T)]
        rows.append(jnp.zeros((SUBLANES - DEC_T, D_MODEL), F32))
        o_ref[...] = jnp.concatenate(rows, axis=0)


def _decode_attention(page_table, q, k_new, v_new, lf_new, cache_k, cache_v, cache_lf, layer):
    nb, n_pages = page_table.shape

    def new_map(b, pg, pt):
        return (b, 0, 0)

    def page_map(b, pg, pt):
        return (layer, pt[b, n_pages - 1 - pg], 0, 0)

    new_spec = pl.BlockSpec((None, SUBLANES, D_MODEL), new_map)
    col = pltpu.VMEM((DEC_ROWS, 1), F32)
    grid_spec = pltpu.PrefetchScalarGridSpec(
        num_scalar_prefetch=1,
        grid=(nb, n_pages),
        in_specs=[
            new_spec, new_spec, new_spec,
            pl.BlockSpec((None, SUBLANES, N_HEADS), new_map),
            pl.BlockSpec((None, None, PAGE_SIZE, D_MODEL), page_map),
            pl.BlockSpec((None, None, PAGE_SIZE, D_MODEL), page_map),
            pl.BlockSpec((None, None, PAGE_SIZE, N_HEADS), page_map),
        ],
        out_specs=new_spec,
        scratch_shapes=[col, col, pltpu.VMEM((DEC_ROWS, D_MODEL), F32),
                        pltpu.VMEM((N_HEADS, 1), F32),
                        pltpu.VMEM((DEC_ROWS, D_MODEL), BF16), col],
    )
    return pl.pallas_call(
        _decode_kernel,
        grid_spec=grid_spec,
        out_shape=jax.ShapeDtypeStruct((nb, SUBLANES, D_MODEL), F32),
        compiler_params=_params(2),
        name="fox_sample_attention",
    )(page_table, q, k_new, v_new, lf_new, cache_k, cache_v, cache_lf)


OUT_TILE = 512


def _oproj_kernel(o_ref, w_ref, h_ref, out_ref, *, head_major):
    if head_major:
        o = jnp.concatenate([o_ref[h] for h in range(N_HEADS)], axis=-1)
    else:
        o = o_ref[...].astype(BF16)
    out_ref[...] = h_ref[...] + _dot(o, w_ref[...])


def _oproj(o, w, h, *, tm, head_major):
    m = h.shape[0]
    if head_major:
        o_spec = pl.BlockSpec((N_HEADS, tm, HEAD_DIM), lambda i, j: (0, i, 0))
    else:
        o_spec = pl.BlockSpec((tm, D_MODEL), lambda i, j: (i, 0))
    return pl.pallas_call(
        functools.partial(_oproj_kernel, head_major=head_major),
        grid=(m // tm, D_MODEL // OUT_TILE),
        in_specs=[o_spec,
                  pl.BlockSpec((D_MODEL, OUT_TILE), lambda i, j: (0, j)),
                  pl.BlockSpec((tm, OUT_TILE), lambda i, j: (i, j))],
        out_specs=pl.BlockSpec((tm, OUT_TILE), lambda i, j: (i, j)),
        out_shape=jax.ShapeDtypeStruct((m, D_MODEL), F32),
        compiler_params=_params(2),
        name="fox_out_proj",
    )(o, w, h)


def _ffn_kernel(h_ref, g_ref, wu_ref, wv_ref, cw_ref, cb_ref, wd_ref, prev_ref,
                out_ref, tail_ref, xn_s, acc_s, ue_s, carry_s,
                *, tm, shift, keep, tiles_per_seq):
    i = pl.program_id(0)
    j = pl.program_id(1)

    @pl.when(j == 0)
    def _():
        xn_s[...] = _rms(h_ref[...], g_ref[...]).astype(BF16)
        acc_s[...] = jnp.zeros(acc_s.shape, F32)

    @pl.when(i % tiles_per_seq == 0)
    def _():
        carry_s[j] = prev_ref[...]

    xn = xn_s[...]
    u = _dot(xn, wu_ref[...])
    v = _dot(xn, wv_ref[...])
    ue_s[0:keep, :] = carry_s[j]
    ue_s[keep:keep + tm, :] = u
    cw = cw_ref[...]
    c = cb_ref[...]
    c = c + cw[0:1, :] * ue_s[keep - 2 * shift:keep - 2 * shift + tm, :]
    c = c + cw[1:2, :] * ue_s[keep - shift:keep - shift + tm, :]
    c = c + cw[2:3, :] * u
    gelu = 0.5 * c * (1.0 + jnp.tanh(0.7978845608028654 * (c + 0.044715 * (c * c * c))))
    acc_s[...] += _dot((gelu * v).astype(BF16), wd_ref[...])
    tail = ue_s[tm:tm + keep, :]
    carry_s[j] = tail
    tail_ref[...] = tail

    @pl.when(j == pl.num_programs(1) - 1)
    def _():
        out_ref[...] = h_ref[...] + acc_s[...]


def _ffn_layer(h, g, w_u, w_v, conv_w, conv_b, w_d, prev0, *, tm, shift, seq_rows):
    m = h.shape[0]
    ff = w_u.shape[1]
    nj = ff // FF_TILE
    keep = prev0.shape[1]
    tps = seq_rows // tm
    assert keep >= CONV_CTX * shift and tm >= keep and keep % SUBLANES == 0
    kern = functools.partial(_ffn_kernel, tm=tm, shift=shift, keep=keep, tiles_per_seq=tps)
    return pl.pallas_call(
        kern,
        grid=(m // tm, nj),
        in_specs=[
            pl.BlockSpec((tm, D_MODEL), lambda i, j: (i, 0)),
            pl.BlockSpec((1, D_MODEL), lambda i, j: (0, 0)),
            pl.BlockSpec((D_MODEL, FF_TILE), lambda i, j: (0, j)),
            pl.BlockSpec((D_MODEL, FF_TILE), lambda i, j: (0, j)),
            pl.BlockSpec((CONV_WIDTH, FF_TILE), lambda i, j: (0, j)),
            pl.BlockSpec((1, FF_TILE), lambda i, j: (0, j)),
            pl.BlockSpec((FF_TILE, D_MODEL), lambda i, j: (j, 0)),
            pl.BlockSpec((None, keep, FF_TILE), lambda i, j: (i // tps, 0, j)),
        ],
        out_specs=[
            pl.BlockSpec((tm, D_MODEL), lambda i, j: (i, 0)),
            pl.BlockSpec((None, keep, FF_TILE), lambda i, j: (i, 0, j)),
        ],
        out_shape=[jax.ShapeDtypeStruct((m, D_MODEL), F32),
                   jax.ShapeDtypeStruct((m // tm, keep, ff), F32)],
        scratch_shapes=[pltpu.VMEM((tm, D_MODEL), BF16),
                        pltpu.VMEM((tm, D_MODEL), F32),
                        pltpu.VMEM((keep + tm, FF_TILE), F32),
                        pltpu.VMEM((nj, keep, FF_TILE), F32)],
        compiler_params=_params(2),
        name="conv_ffn",
    )(h, g, w_u, w_v, conv_w, conv_b, w_d, prev0)


def _ple_kernel(h_ref, hres_ref, g_ref, p_ref, wpe_ref, wpg_ref, out_ref, xn_s, p_s):
    j = pl.program_id(1)

    @pl.when(j == 0)
    def _():
        xn_s[...] = _rms(h_ref[...], g_ref[...]).astype(BF16)
        p_s[...] = p_ref[...].astype(BF16)

    e = _dot(p_s[...], wpe_ref[...])
    z = _dot(xn_s[...], wpg_ref[...])
    out_ref[...] = hres_ref[...] + e * jax.nn.sigmoid(z)


def _ple_layer(h, g, p, w_pe, w_pg, *, tm):
    m = h.shape[0]
    pd = p.shape[1]
    return pl.pallas_call(
        _ple_kernel,
        grid=(m // tm, D_MODEL // OUT_TILE),
        in_specs=[
            pl.BlockSpec((tm, D_MODEL), lambda i, j: (i, 0)),
            pl.BlockSpec((tm, OUT_TILE), lambda i, j: (i, j)),
            pl.BlockSpec((1, D_MODEL), lambda i, j: (0, 0)),
            pl.BlockSpec((tm, pd), lambda i, j: (i, 0)),
            pl.BlockSpec((pd, OUT_TILE), lambda i, j: (0, j)),
            pl.BlockSpec((D_MODEL, OUT_TILE), lambda i, j: (0, j)),
        ],
        out_specs=pl.BlockSpec((tm, OUT_TILE), lambda i, j: (i, j)),
        out_shape=jax.ShapeDtypeStruct((m, D_MODEL), F32),
        scratch_shapes=[pltpu.VMEM((tm, D_MODEL), BF16), pltpu.VMEM((tm, pd), BF16)],
        compiler_params=_params(2),
        name="ple_gate",
    )(h, h, g, p, w_pe, w_pg)


def _norm_kernel(h_ref, g_ref, o_ref):
    o_ref[...] = _rms(h_ref[...], g_ref[...])


def _final_norm(h, g, *, tm):
    m = h.shape[0]
    return pl.pallas_call(
        _norm_kernel,
        grid=(m // tm,),
        in_specs=[pl.BlockSpec((tm, D_MODEL), lambda i: (i, 0)),
                  pl.BlockSpec((1, D_MODEL), lambda i: (0, 0))],
        out_specs=pl.BlockSpec((tm, D_MODEL), lambda i: (i, 0)),
        out_shape=jax.ShapeDtypeStruct((m, D_MODEL), F32),
        compiler_params=_params(1),
        name="final_norm",
    )(h, g)


PROMPT_TILE = 512
FLASH_TILE = 512


def _trunk(h, p, wts, *, batch, steps, time_major, pos0, pool_prev, conv_prev, decode):
    depth = p.shape[0]
    m = h.shape[0]
    if time_major:
        tm, shift, seq_rows = m, batch, m
    else:
        tm, shift, seq_rows = PROMPT_TILE, 1, steps
    pool_tails, conv_tails, ks, vs, lfs = [], [], [], [], []
    for i in range(depth):
        j = i // 2
        if i % 2 == 0:
            h, tail = _pool_layer(h, wts["g_mix"][i], pool_prev[j], wts["w_pool"][j],
                                  wts["pool_scale"][j], tm=tm, shift=shift,
                                  seq_rows=seq_rows, pos0=pos0)
            pool_tails.append(tail)
        else:
            qh, kh, vh, k, v, lf = _qkvf(h, wts["g_mix"][i], wts["w_qkv"][j], wts["w_f"][j],
                                         wts["b_f"][j], tm=tm)
            ks.append(k)
            vs.append(v)
            lfs.append(lf)
            if decode is None:
                lf_t = lf.reshape(batch, steps, N_HEADS).transpose(0, 2, 1)
                ct = _cumsum_last(lf_t)
                c = ct.transpose(0, 2, 1)
                o = _flash_prompt(qh, kh, vh, c, ct, batch=batch, seq=steps, tq=FLASH_TILE)
                h = _oproj(o, wts["w_o"][j], h, tm=tm, head_major=True)
            else:
                page_table, cache_k, cache_v, cache_lf = decode

                def per_seq(a):
                    a = a.reshape(steps, batch, -1).transpose(1, 0, 2)
                    return jnp.pad(a, ((0, 0), (0, SUBLANES - steps), (0, 0)))

                q_flat = qh.transpose(1, 0, 2).reshape(m, D_MODEL).astype(F32)
                o = _decode_attention(page_table, per_seq(q_flat), per_seq(k), per_seq(v),
                                      per_seq(lf), cache_k, cache_v, cache_lf, j)
                o = o[:, :steps].transpose(1, 0, 2).reshape(m, D_MODEL)
                h = _oproj(o, wts["w_o"][j], h, tm=tm, head_major=False)
        h, ctail = _ffn_layer(h, wts["g_ffn"][i], wts["w_u"][i], wts["w_v"][i], wts["conv_w"][i],
                              wts["conv_b"][i], wts["w_d"][i], conv_prev[i],
                              tm=tm, shift=shift, seq_rows=seq_rows)
        conv_tails.append(ctail)
        h = _ple_layer(h, wts["g_ple"][i], p[i], wts["w_pe"][i], wts["w_pg"][i], tm=tm)
    y = _final_norm(h, wts["g_final"], tm=tm)
    return y, pool_tails, conv_tails, ks, vs, lfs


def kernel(x_prompt, x_sample, state_pool, state_conv, cache_k, cache_v, cache_lf, page_table,
           p_prompt, p_sample, g_mix, w_pool, pool_scale, w_qkvf, b_f, w_o,
           g_ffn, w_up, conv_w, conv_b, w_down, w_pe, g_ple, w_pg, g_final):
    depth = g_mix.shape[0]
    bp, sp, _ = x_prompt.shape
    bs, ts, _ = x_sample.shape
    d_ff = w_down.shape[1]
    ff_pad = pl.cdiv(d_ff, FF_TILE) * FF_TILE
    padc = ff_pad - d_ff
    n_pool = w_pool.shape[0]
    n_fox = w_o.shape[0]
    assert ts == DEC_T

    wts = {
        "g_mix": g_mix[:, None, :], "g_ffn": g_ffn[:, None, :], "g_ple": g_ple[:, None, :],
        "g_final": g_final[None, :],
        "w_pool": w_pool.astype(BF16), "pool_scale": pool_scale[:, None, :],
        "w_qkv": w_qkvf[:, :, :3 * D_MODEL].astype(BF16),
        "w_f": w_qkvf[:, :, 3 * D_MODEL:].astype(BF16),
        "b_f": b_f[:, None, :],
        "w_o": w_o.astype(BF16),
        "w_u": jnp.pad(w_up[:, :, :d_ff].astype(BF16), ((0, 0), (0, 0), (0, padc))),
        "w_v": jnp.pad(w_up[:, :, d_ff:].astype(BF16), ((0, 0), (0, 0), (0, padc))),
        "conv_w": jnp.pad(conv_w, ((0, 0), (0, 0), (0, padc))),
        "conv_b": jnp.pad(conv_b, ((0, 0), (0, padc)))[:, None, :],
        "w_d": jnp.pad(w_down.astype(BF16), ((0, 0), (0, padc), (0, 0))),
        "w_pe": w_pe.astype(BF16), "w_pg": w_pg.astype(BF16),
    }

    pool_prev_p = jnp.zeros((n_pool, bp, POOL_SPAN, D_MODEL), F32)
    conv_prev_p = jnp.zeros((depth, bp, SUBLANES, ff_pad), F32)
    y_p, pool_p, conv_p, k_p, v_p, lf_p = _trunk(
        x_prompt.reshape(bp * sp, D_MODEL), p_prompt.reshape(depth, bp * sp, -1), wts,
        batch=bp, steps=sp, time_major=False, pos0=0,
        pool_prev=pool_prev_p, conv_prev=conv_prev_p, decode=None)

    n_pages = page_table.shape[1]
    pool_prev_s = jnp.pad(state_pool.transpose(0, 2, 1, 3), ((0, 0), (1, 0), (0, 0), (0, 0)))
    pool_prev_s = pool_prev_s.reshape(n_pool, 1, POOL_SPAN * bs, D_MODEL)
    conv_prev_s = jnp.pad(state_conv.transpose(0, 2, 1, 3), ((0, 0), (0, 0), (0, 0), (0, padc)))
    conv_prev_s = conv_prev_s.reshape(depth, 1, CONV_CTX * bs, ff_pad)
    n_phys = cache_k.shape[1]
    decode = (page_table,
              cache_k.reshape(n_fox, n_phys, PAGE_SIZE, D_MODEL),
              cache_v.reshape(n_fox, n_phys, PAGE_SIZE, D_MODEL),
              cache_lf)
    y_s, pool_s, conv_s, k_s, v_s, lf_s = _trunk(
        x_sample.transpose(1, 0, 2).reshape(ts * bs, D_MODEL),
        p_sample.transpose(0, 2, 1, 3).reshape(depth, ts * bs, -1), wts,
        batch=bs, steps=ts, time_major=True, pos0=n_pages * PAGE_SIZE,
        pool_prev=pool_prev_s, conv_prev=conv_prev_s, decode=decode)

    def from_tm(a, lead):
        return a.reshape(lead, bs, -1).transpose(1, 0, 2)

    ctx = POOL_SPAN - 1
    out_pool_p = jnp.stack([t[:, 1:, :] for t in pool_p])
    out_pool_s = jnp.stack([from_tm(t[0], POOL_SPAN)[:, 1:, :] for t in pool_s])
    tps = sp // PROMPT_TILE
    out_conv_p = jnp.stack([t[tps - 1::tps, SUBLANES - CONV_CTX:, :d_ff] for t in conv_p])
    out_conv_s = jnp.stack([from_tm(t[0], CONV_CTX)[:, :, :d_ff] for t in conv_s])
    assert out_pool_p.shape[2] == ctx

    def heads_p(a):
        return a.reshape(bp, sp, N_HEADS, HEAD_DIM)

    def heads_s(a):
        return from_tm(a, ts).reshape(bs, ts, N_HEADS, HEAD_DIM)

    return (y_p.reshape(bp, sp, D_MODEL),
            from_tm(y_s, ts),
            out_pool_p, out_pool_s, out_conv_p, out_conv_s,
            jnp.stack([heads_p(a) for a in k_p]),
            jnp.stack([heads_p(a) for a in v_p]),
            jnp.stack([a.reshape(bp, sp, N_HEADS) for a in lf_p]),
            jnp.stack([heads_s(a) for a in k_s]),
            jnp.stack([heads_s(a) for a in v_s]),
            jnp.stack([from_tm(a, ts) for a in lf_s]))
```

```python
import functools

import jax
import jax.numpy as jnp
from jax import lax
from jax.experimental import pallas as pl
from jax.experimental.pallas import tpu as pltpu

D_MODEL = 2048
N_HEADS = 16
HEAD_DIM = D_MODEL // N_HEADS
POOL_WINDOWS = (2, 4, 8, 16)
POOL_GROUP_DIM = D_MODEL // len(POOL_WINDOWS)
POOL_SPAN = max(POOL_WINDOWS)
CONV_WIDTH = 3
CONV_CTX = CONV_WIDTH - 1
PAGE_SIZE = 128
RMS_EPS = 1e-6
ATTN_SCALE = HEAD_DIM ** -0.5
LOG2E = 1.4426950408889634
Q_SCALE = ATTN_SCALE * LOG2E

LANES = 128
SUBLANES = 8
FF_TILE = 512
VMEM_LIMIT = 56 * 1024 * 1024

F32 = jnp.float32
BF16 = jnp.bfloat16
NEG_INF = float("-inf")


def _params(n_axes):
    return pltpu.CompilerParams(dimension_semantics=("arbitrary",) * n_axes,
                                vmem_limit_bytes=VMEM_LIMIT)


def _rms(x, g):
    return x * lax.rsqrt(jnp.mean(x * x, axis=-1, keepdims=True) + RMS_EPS) * g


def _dot(a, b):
    return jnp.dot(a, b, preferred_element_type=F32)


def _dot_nt(a, b):
    return lax.dot_general(a, b, (((1,), (1,)), ((), ())), preferred_element_type=F32)


def _pool_kernel(h_ref, g_ref, prev_ref, w_ref, sc_ref, o_ref, tail_ref, ext_s,
                 *, tm, shift, tiles_per_seq, pos0):
    i = pl.program_id(0)
    span = POOL_SPAN * shift
    it = i % tiles_per_seq

    @pl.when(it == 0)
    def _():
        ext_s[0:span, :] = prev_ref[...]

    x = h_ref[...]
    xn = _rms(x, g_ref[...])
    ext_s[span:span + tm, :] = xn
    row = lax.broadcasted_iota(jnp.int32, (tm, 1), 0)
    pos = pos0 + (it * tm + row) // shift
    for g, w in enumerate(POOL_WINDOWS):
        cols = slice(g * POOL_GROUP_DIM, (g + 1) * POOL_GROUP_DIM)
        acc = xn[:, cols]
        for k in range(1, w):
            acc = acc + ext_s[span - k * shift:span - k * shift + tm, cols]
        cnt = jnp.minimum(w, pos + 1).astype(F32)
        delta = acc / cnt - xn[:, cols]
        y = _dot(delta.astype(BF16), w_ref[g])
        o_ref[:, cols] = x[:, cols] + y * sc_ref[:, cols]
    tail = ext_s[tm:tm + span, :]
    tail_ref[...] = tail
    if tiles_per_seq > 1:
        ext_s[0:span, :] = tail


def _pool_layer(h, g, prev0, w_pool, scale, *, tm, shift, seq_rows, pos0):
    m = h.shape[0]
    span = POOL_SPAN * shift
    tps = seq_rows // tm
    nseq = m // seq_rows
    assert tps == 1 or tm >= span
    kern = functools.partial(_pool_kernel, tm=tm, shift=shift, tiles_per_seq=tps, pos0=pos0)
    return pl.pallas_call(
        kern,
        grid=(m // tm,),
        in_specs=[
            pl.BlockSpec((tm, D_MODEL), lambda i: (i, 0)),
            pl.BlockSpec((1, D_MODEL), lambda i: (0, 0)),
            pl.BlockSpec((None, span, D_MODEL), lambda i: (i // tps, 0, 0)),
            pl.BlockSpec((len(POOL_WINDOWS), POOL_GROUP_DIM, POOL_GROUP_DIM), lambda i: (0, 0, 0)),
            pl.BlockSpec((1, D_MODEL), lambda i: (0, 0)),
        ],
        out_specs=[
            pl.BlockSpec((tm, D_MODEL), lambda i: (i, 0)),
            pl.BlockSpec((None, span, D_MODEL), lambda i: (i // tps, 0, 0)),
        ],
        out_shape=[jax.ShapeDtypeStruct((m, D_MODEL), F32),
                   jax.ShapeDtypeStruct((nseq, span, D_MODEL), F32)],
        scratch_shapes=[pltpu.VMEM((span + tm, D_MODEL), F32)],
        compiler_params=_params(1),
        name="pool_layer",
    )(h, g, prev0, w_pool, scale)


QKV_TILE = 512
HEADS_PER_TILE = QKV_TILE // HEAD_DIM
TILES_PER_PROJ = D_MODEL // QKV_TILE


def _qkvf_kernel(h_ref, g_ref, w_ref, wf_ref, bf_ref,
                 qh_ref, kh_ref, vh_ref, k_ref, v_ref, lf_ref, xn_s):
    j = pl.program_id(1)

    @pl.when(j == 0)
    def _():
        xn = _rms(h_ref[...], g_ref[...]).astype(BF16)
        xn_s[...] = xn
        z = _dot(xn, wf_ref[...]) + bf_ref[...]
        lf_ref[...] = jnp.minimum(z, 0.0) - jnp.log1p(jnp.exp(-jnp.abs(z)))

    res = _dot(xn_s[...], w_ref[...])

    def heads(dst, scale=None):
        for hh in range(HEADS_PER_TILE):
            x = res[:, hh * HEAD_DIM:(hh + 1) * HEAD_DIM]
            dst[hh] = (x if scale is None else x * scale).astype(BF16)

    @pl.when(j < TILES_PER_PROJ)
    def _():
        heads(qh_ref, Q_SCALE)

    @pl.when((j >= TILES_PER_PROJ) & (j < 2 * TILES_PER_PROJ))
    def _():
        heads(kh_ref)
        k_ref[...] = res

    @pl.when(j >= 2 * TILES_PER_PROJ)
    def _():
        heads(vh_ref)
        v_ref[...] = res


def _qkvf(h, g, w_qkv, w_f, b_f, *, tm):
    m = h.shape[0]
    tpp = TILES_PER_PROJ

    def sel(which):
        return lambda i, j: jnp.clip(j - which * tpp, 0, tpp - 1)

    hm_shape = jax.ShapeDtypeStruct((N_HEADS, m, HEAD_DIM), BF16)
    flat_shape = jax.ShapeDtypeStruct((m, D_MODEL), F32)

    def hm_spec(which):
        s = sel(which)
        return pl.BlockSpec((HEADS_PER_TILE, tm, HEAD_DIM), lambda i, j: (s(i, j), i, 0))

    def flat_spec(which):
        s = sel(which)
        return pl.BlockSpec((tm, QKV_TILE), lambda i, j: (i, s(i, j)))

    return pl.pallas_call(
        _qkvf_kernel,
        grid=(m // tm, 3 * tpp),
        in_specs=[
            pl.BlockSpec((tm, D_MODEL), lambda i, j: (i, 0)),
            pl.BlockSpec((1, D_MODEL), lambda i, j: (0, 0)),
            pl.BlockSpec((D_MODEL, QKV_TILE), lambda i, j: (0, j)),
            pl.BlockSpec((D_MODEL, N_HEADS), lambda i, j: (0, 0)),
            pl.BlockSpec((1, N_HEADS), lambda i, j: (0, 0)),
        ],
        out_specs=[hm_spec(0), hm_spec(1), hm_spec(2), flat_spec(1), flat_spec(2),
                   pl.BlockSpec((tm, N_HEADS), lambda i, j: (i, 0))],
        out_shape=[hm_shape, hm_shape, hm_shape, flat_shape, flat_shape,
                   jax.ShapeDtypeStruct((m, N_HEADS), F32)],
        scratch_shapes=[pltpu.VMEM((tm, D_MODEL), BF16)],
        compiler_params=_params(2),
        name="fox_qkvf",
    )(h, g, w_qkv, w_f, b_f)


def _lane_cumsum(x):
    n = x.shape[-1]
    lane = lax.broadcasted_iota(jnp.int32, x.shape, x.ndim - 1)
    k = 1
    while k < n:
        x = x + jnp.where(lane >= k, pltpu.roll(x, k, x.ndim - 1), 0.0)
        k *= 2
    return x


def _cumsum_kernel(x_ref, o_ref):
    o_ref[...] = _lane_cumsum(x_ref[...]) * LOG2E


def _cumsum_last(x):
    b, hh, s = x.shape
    return pl.pallas_call(
        _cumsum_kernel,
        grid=(b,),
        in_specs=[pl.BlockSpec((None, hh, s), lambda i: (i, 0, 0))],
        out_specs=pl.BlockSpec((None, hh, s), lambda i: (i, 0, 0)),
        out_shape=jax.ShapeDtypeStruct(x.shape, F32),
        compiler_params=_params(1),
        name="forget_cumsum",
    )(x)


def _flash_kernel(q_ref, k_ref, v_ref, c_ref, ct_ref, o_ref, m_s, l_s, acc_s, cq_s, *, tq):
    qi = pl.program_id(1)
    ki = pl.program_id(2)

    @pl.when(ki == 0)
    def _():
        m_s[...] = jnp.full(m_s.shape, NEG_INF, F32)
        l_s[...] = jnp.zeros(l_s.shape, F32)
        acc_s[...] = jnp.zeros(acc_s.shape, F32)
        c = c_ref[...]
        for h in range(N_HEADS):
            cq_s[h] = c[:, h:h + 1]

    def step(diagonal):
        def body(h, carry):
            s = _dot_nt(q_ref[h], k_ref[h])
            s = s + cq_s[h] - ct_ref[pl.ds(h, 1), :]
            if diagonal:
                row = lax.broadcasted_iota(jnp.int32, (tq, tq), 0)
                col = lax.broadcasted_iota(jnp.int32, (tq, tq), 1)
                s = jnp.where(col <= row, s, NEG_INF)
            m_prev = m_s[h]
            m_new = jnp.maximum(m_prev, jnp.max(s, axis=-1, keepdims=True))
            alpha = jnp.exp2(m_prev - m_new)
            p = jnp.exp2(s - m_new)
            l_s[h] = alpha * l_s[h] + jnp.sum(p, axis=-1, keepdims=True)
            acc_s[h] = alpha * acc_s[h] + _dot(p.astype(BF16), v_ref[h])
            m_s[h] = m_new
            return carry

        lax.fori_loop(0, N_HEADS, body, 0)

    @pl.when(ki < qi)
    def _():
        step(False)

    @pl.when(ki == qi)
    def _():
        step(True)

        def fin(h, carry):
            o_ref[h] = (acc_s[h] / l_s[h]).astype(BF16)
            return carry

        lax.fori_loop(0, N_HEADS, fin, 0)


def _flash_prompt(qh, kh, vh, c, ct, *, batch, seq, tq):
    nq = seq // tq

    def q_map(b, qi, ki):
        return (0, b * nq + qi, 0)

    def kv_map(b, qi, ki):
        return (0, b * nq + jnp.minimum(ki, qi), 0)

    kern = functools.partial(_flash_kernel, tq=tq)
    col = pltpu.VMEM((N_HEADS, tq, 1), F32)
    return pl.pallas_call(
        kern,
        grid=(batch, nq, nq),
        in_specs=[
            pl.BlockSpec((N_HEADS, tq, HEAD_DIM), q_map),
            pl.BlockSpec((N_HEADS, tq, HEAD_DIM), kv_map),
            pl.BlockSpec((N_HEADS, tq, HEAD_DIM), kv_map),
            pl.BlockSpec((None, tq, N_HEADS), lambda b, qi, ki: (b, qi, 0)),
            pl.BlockSpec((None, N_HEADS, tq), lambda b, qi, ki: (b, 0, jnp.minimum(ki, qi))),
        ],
        out_specs=pl.BlockSpec((N_HEADS, tq, HEAD_DIM), q_map),
        out_shape=jax.ShapeDtypeStruct(qh.shape, BF16),
        scratch_shapes=[col, col, pltpu.VMEM((N_HEADS, tq, HEAD_DIM), F32), col],
        compiler_params=_params(3),
        name="fox_prompt_attention",
    )(qh, kh, vh, c, ct)


DEC_T = 4
DEC_ROWS = DEC_T * N_HEADS
PAGE_ROWS = PAGE_SIZE * N_HEADS
LF_ROWS = PAGE_ROWS // LANES
PAGES_PER_STEP = 4


def _row_to_col(r):
    n = r.shape[-1]
    eye = (lax.broadcasted_iota(jnp.int32, (n, n), 0) == lax.broadcasted_iota(jnp.int32, (n, n), 1))
    return jnp.sum(jnp.where(eye, jnp.broadcast_to(r, (n, n)), 0.0), axis=1, keepdims=True)


def _strided_prefix(x, lane, steps):
    for k in steps:
        x = x + jnp.where(lane >= k, pltpu.roll(x, k, 1), 0.0)
    return x


def _decode_kernel(pt_ref, q_ref, kn_ref, vn_ref, lfn_ref, *rest, pps):
    k_refs, v_refs, lf_refs = rest[:pps], rest[pps:2 * pps], rest[2 * pps:3 * pps]
    o_ref, m_s, l_s, acc_s, carry_s, mask_s = rest[3 * pps:]
    pg = pl.program_id(1)
    lane = lax.broadcasted_iota(jnp.int32, (1, LANES), 1)
    q = q_ref[...]

    @pl.when(pg == 0)
    def _():
        cn_row = _strided_prefix(lfn_ref[...] * LOG2E, lane, (N_HEADS, 2 * N_HEADS))
        cn_col = _row_to_col(cn_row)[:DEC_ROWS]
        row_h = lax.broadcasted_iota(jnp.int32, (DEC_ROWS, PAGE_ROWS), 0) % N_HEADS
        col_h = lax.broadcasted_iota(jnp.int32, (DEC_ROWS, PAGE_ROWS), 1) % N_HEADS
        mask_s[...] = jnp.where(row_h == col_h, 0.0, NEG_INF) + cn_col
        s = _dot_nt(q, kn_ref[...])
        r = lax.broadcasted_iota(jnp.int32, (DEC_ROWS, LANES), 0)
        c = lax.broadcasted_iota(jnp.int32, (DEC_ROWS, LANES), 1)
        valid = (c < DEC_ROWS) & (c % N_HEADS == r % N_HEADS) & (c // N_HEADS <= r // N_HEADS)
        s = jnp.where(valid, s + (cn_col - cn_row), NEG_INF)
        m0 = jnp.max(s, axis=-1, keepdims=True)
        p0 = jnp.exp2(s - m0)
        m_s[...] = m0
        l_s[...] = jnp.sum(p0, axis=-1, keepdims=True)
        acc_s[...] = _dot(p0.astype(BF16), vn_ref[...])
        carry_s[...] = jnp.zeros(carry_s.shape, F32)

    sub = lax.broadcasted_iota(jnp.int32, (LF_ROWS, LANES), 0)
    lane_steps = (N_HEADS, 2 * N_HEADS, 4 * N_HEADS)
    carry = carry_s[...]
    logits = []
    for u in range(pps):
        x = _strided_prefix(lf_refs[u][...] * LOG2E, lane, lane_steps)
        y = jnp.where(lane >= LANES - N_HEADS, x, 0.0)
        for k in lane_steps:
            y = y + pltpu.roll(y, k, 1)
        e = y
        k = 1
        while k < LF_ROWS:
            e = e + jnp.where(sub >= k, pltpu.roll(e, k, 0), 0.0)
            k *= 2
        tot = e[LF_ROWS - 1:LF_ROWS, :]
        suffix = (tot - (x + (e - y))) + carry
        carry = carry + tot
        bias = jnp.concatenate([suffix[r:r + 1, :] for r in range(LF_ROWS)], axis=1)
        kp = k_refs[u][...].reshape(PAGE_ROWS, HEAD_DIM).astype(BF16)
        logits.append(_dot_nt(q, kp) + bias + mask_s[...])
    carry_s[...] = carry

    m_prev = m_s[...]
    m_new = m_prev
    for s in logits:
        m_new = jnp.maximum(m_new, jnp.max(s, axis=-1, keepdims=True))
    alpha = jnp.exp2(m_prev - m_new)
    l_new = alpha * l_s[...]
    acc = alpha * acc_s[...]
    for u in range(pps):
        p = jnp.exp2(logits[u] - m_new)
        l_new = l_new + jnp.sum(p, axis=-1, keepdims=True)
        vp = v_refs[u][...].reshape(PAGE_ROWS, HEAD_DIM).astype(BF16)
        acc = acc + _dot(p.astype(BF16), vp)
    m_s[...] = m_new
    l_s[...] = l_new
    acc_s[...] = acc

    @pl.when(pg == pl.num_programs(1) - 1)
    def _():
        o_ref[...] = acc / l_new


def _decode_attention(page_table, q, k_new, v_new, lf_new, cache_k, cache_v, cache_lf, layer):
    nb, n_pages = page_table.shape
    pps = PAGES_PER_STEP
    assert n_pages % pps == 0

    def new_map(b, pg, pt):
        return (b, 0, 0)

    def page_spec(block, u):
        zeros = (0,) * (len(block) - 2)
        return pl.BlockSpec(
            block, lambda b, pg, pt: (layer, pt[b, n_pages - 1 - (pg * pps + u)]) + zeros)

    kv_block = (None, None, PAGE_SIZE, N_HEADS, HEAD_DIM)
    lf_block = (None, None, LF_ROWS, LANES)
    col = pltpu.VMEM((DEC_ROWS, 1), F32)
    grid_spec = pltpu.PrefetchScalarGridSpec(
        num_scalar_prefetch=1,
        grid=(nb, n_pages // pps),
        in_specs=[
            pl.BlockSpec((None, DEC_ROWS, HEAD_DIM), new_map),
            pl.BlockSpec((None, LANES, HEAD_DIM), new_map),
            pl.BlockSpec((None, LANES, HEAD_DIM), new_map),
            pl.BlockSpec((None, 1, LANES), new_map),
            *[page_spec(kv_block, u) for u in range(pps)],
            *[page_spec(kv_block, u) for u in range(pps)],
            *[page_spec(lf_block, u) for u in range(pps)],
        ],
        out_specs=pl.BlockSpec((None, DEC_ROWS, HEAD_DIM), new_map),
        scratch_shapes=[col, col, pltpu.VMEM((DEC_ROWS, HEAD_DIM), F32),
                        pltpu.VMEM((1, LANES), F32),
                        pltpu.VMEM((DEC_ROWS, PAGE_ROWS), F32)],
    )
    return pl.pallas_call(
        functools.partial(_decode_kernel, pps=pps),
        grid_spec=grid_spec,
        out_shape=jax.ShapeDtypeStruct((nb, DEC_ROWS, HEAD_DIM), F32),
        compiler_params=_params(2),
        name="fox_sample_attention",
    )(page_table, q, k_new, v_new, lf_new,
      *([cache_k] * pps), *([cache_v] * pps), *([cache_lf] * pps))


OUT_TILE = 512


def _oproj_kernel(o_ref, w_ref, h_ref, out_ref, *, head_major):
    if head_major:
        o = jnp.concatenate([o_ref[h] for h in range(N_HEADS)], axis=-1)
    else:
        o = o_ref[...].astype(BF16)
    out_ref[...] = h_ref[...] + _dot(o, w_ref[...])


def _oproj(o, w, h, *, tm, head_major):
    m = h.shape[0]
    if head_major:
        o_spec = pl.BlockSpec((N_HEADS, tm, HEAD_DIM), lambda i, j: (0, i, 0))
    else:
        o_spec = pl.BlockSpec((tm, D_MODEL), lambda i, j: (i, 0))
    return pl.pallas_call(
        functools.partial(_oproj_kernel, head_major=head_major),
        grid=(m // tm, D_MODEL // OUT_TILE),
        in_specs=[o_spec,
                  pl.BlockSpec((D_MODEL, OUT_TILE), lambda i, j: (0, j)),
                  pl.BlockSpec((tm, OUT_TILE), lambda i, j: (i, j))],
        out_specs=pl.BlockSpec((tm, OUT_TILE), lambda i, j: (i, j)),
        out_shape=jax.ShapeDtypeStruct((m, D_MODEL), F32),
        compiler_params=_params(2),
        name="fox_out_proj",
    )(o, w, h)


N_OUT_TILES = D_MODEL // OUT_TILE


def _ffn_ple_kernel(h_ref, g_ref, wu_ref, wv_ref, cw_ref, cb_ref, wd_ref, prev_ref,
                    gp_ref, p_ref, wpe_ref, wpg_ref, gf_ref,
                    out_ref, tail_ref, xn_s, p_s, ue_s, carry_s,
                    *, tm, shift, keep, tiles_per_seq, nj, final_norm):
    i = pl.program_id(0)
    j = pl.program_id(1)

    @pl.when(j == 0)
    def _():
        xn_s[...] = _rms(h_ref[...], g_ref[...]).astype(BF16)
        out_ref[...] = h_ref[...]

    @pl.when(j < nj)
    def _():
        @pl.when(i % tiles_per_seq == 0)
        def _():
            carry_s[j] = prev_ref[...]

        xn = xn_s[...]
        u = _dot(xn, wu_ref[...])
        v = _dot(xn, wv_ref[...])
        ue_s[0:keep, :] = carry_s[j]
        ue_s[keep:keep + tm, :] = u
        cw = cw_ref[...]
        c = cb_ref[...]
        c = c + cw[0:1, :] * ue_s[keep - 2 * shift:keep - 2 * shift + tm, :]
        c = c + cw[1:2, :] * ue_s[keep - shift:keep - shift + tm, :]
        c = c + cw[2:3, :] * u
        gelu = 0.5 * c * (1.0 + jnp.tanh(0.7978845608028654 * (c + 0.044715 * (c * c * c))))
        out_ref[...] += _dot((gelu * v).astype(BF16), wd_ref[...])
        tail = ue_s[tm:tm + keep, :]
        carry_s[j] = tail
        tail_ref[...] = tail

    @pl.when(j == nj - 1)
    def _():
        xn_s[...] = _rms(out_ref[...], gp_ref[...]).astype(BF16)
        p_s[...] = p_ref[...].astype(BF16)

    for n in range(N_OUT_TILES):
        @pl.when(j == nj + n)
        def _(n=n):
            cols = slice(n * OUT_TILE, (n + 1) * OUT_TILE)
            e = _dot(p_s[...], wpe_ref[...])
            z = _dot(xn_s[...], wpg_ref[...])
            out_ref[:, cols] = out_ref[:, cols] + e * jax.nn.sigmoid(z)
            if final_norm and n == N_OUT_TILES - 1:
                out_ref[...] = _rms(out_ref[...], gf_ref[...])


def _ffn_ple_layer(h, g, w_u, w_v, conv_w, conv_b, w_d, prev0, g_ple, p, w_pe, w_pg, g_final,
                   *, tm, shift, seq_rows, final_norm):
    m = h.shape[0]
    ff = w_u.shape[1]
    pd = p.shape[1]
    nj = ff // FF_TILE
    keep = prev0.shape[1]
    tps = seq_rows // tm
    assert keep >= CONV_CTX * shift and tm >= keep and keep % SUBLANES == 0
    kern = functools.partial(_ffn_ple_kernel, tm=tm, shift=shift, keep=keep, tiles_per_seq=tps,
                             nj=nj, final_norm=final_norm)

    def ffj(j):
        return jnp.minimum(j, nj - 1)

    def plej(j):
        return jnp.clip(j - nj, 0, N_OUT_TILES - 1)

    row = pl.BlockSpec((1, D_MODEL), lambda i, j: (0, 0))
    return pl.pallas_call(
        kern,
        grid=(m // tm, nj + N_OUT_TILES),
        in_specs=[
            pl.BlockSpec((tm, D_MODEL), lambda i, j: (i, 0)),
            row,
            pl.BlockSpec((D_MODEL, FF_TILE), lambda i, j: (0, ffj(j))),
            pl.BlockSpec((D_MODEL, FF_TILE), lambda i, j: (0, ffj(j))),
            pl.BlockSpec((CONV_WIDTH, FF_TILE), lambda i, j: (0, ffj(j))),
            pl.BlockSpec((1, FF_TILE), lambda i, j: (0, ffj(j))),
            pl.BlockSpec((FF_TILE, D_MODEL), lambda i, j: (ffj(j), 0)),
            pl.BlockSpec((None, keep, FF_TILE), lambda i, j: (i // tps, 0, ffj(j))),
            row,
            pl.BlockSpec((tm, pd), lambda i, j: (i, 0)),
            pl.BlockSpec((pd, OUT_TILE), lambda i, j: (0, plej(j))),
            pl.BlockSpec((D_MODEL, OUT_TILE), lambda i, j: (0, plej(j))),
            row,
        ],
        out_specs=[
            pl.BlockSpec((tm, D_MODEL), lambda i, j: (i, 0)),
            pl.BlockSpec((None, keep, FF_TILE), lambda i, j: (i, 0, ffj(j))),
        ],
        out_shape=[jax.ShapeDtypeStruct((m, D_MODEL), F32),
                   jax.ShapeDtypeStruct((m // tm, keep, ff), F32)],
        scratch_shapes=[pltpu.VMEM((tm, D_MODEL), BF16),
                        pltpu.VMEM((tm, pd), BF16),
                        pltpu.VMEM((keep + tm, FF_TILE), F32),
                        pltpu.VMEM((nj, keep, FF_TILE), F32)],
        compiler_params=_params(2),
        name="conv_ffn_ple",
    )(h, g, w_u, w_v, conv_w, conv_b, w_d, prev0, g_ple, p, w_pe, w_pg, g_final)


PROMPT_TILE = 512
QKV_ROWS = 1024
FLASH_TILE = 512


def _trunk(h, p, wts, *, batch, steps, time_major, pos0, pool_prev, conv_prev, decode):
    depth = p.shape[0]
    m = h.shape[0]
    if time_major:
        tm, tm_qkv, shift, seq_rows = m, m, batch, m
    else:
        tm, tm_qkv, shift, seq_rows = PROMPT_TILE, QKV_ROWS, 1, steps
    pool_tails, conv_tails, ks, vs, lfs = [], [], [], [], []
    for i in range(depth):
        j = i // 2
        if i % 2 == 0:
            h, tail = _pool_layer(h, wts["g_mix"][i], pool_prev[j], wts["w_pool"][j],
                                  wts["pool_scale"][j], tm=tm, shift=shift,
                                  seq_rows=seq_rows, pos0=pos0)
            pool_tails.append(tail)
        else:
            qh, kh, vh, k, v, lf = _qkvf(h, wts["g_mix"][i], wts["w_qkv"][j], wts["w_f"][j],
                                         wts["b_f"][j], tm=tm_qkv)
            ks.append(k)
            vs.append(v)
            lfs.append(lf)
            if decode is None:
                lf_t = lf.reshape(batch, steps, N_HEADS).transpose(0, 2, 1)
                ct = _cumsum_last(lf_t)
                c = ct.transpose(0, 2, 1)
                o = _flash_prompt(qh, kh, vh, c, ct, batch=batch, seq=steps, tq=FLASH_TILE)
                h = _oproj(o, wts["w_o"][j], h, tm=tm, head_major=True)
            else:
                page_table, cache_k, cache_v, cache_lf = decode

                def per_seq(a):
                    a = a.reshape(steps, batch, N_HEADS, -1).transpose(1, 0, 2, 3)
                    return a.reshape(batch, steps * N_HEADS, -1)

                def as_page(a):
                    return jnp.pad(a, ((0, 0), (0, LANES - DEC_ROWS), (0, 0))).astype(BF16)

                q_rows = per_seq(qh.transpose(1, 0, 2).reshape(m, D_MODEL))
                lf_rows = per_seq(lf).reshape(batch, 1, DEC_ROWS)
                lf_rows = jnp.pad(lf_rows, ((0, 0), (0, 0), (0, LANES - DEC_ROWS)))
                o = _decode_attention(page_table, q_rows, as_page(per_seq(k)), as_page(per_seq(v)),
                                      lf_rows, cache_k, cache_v, cache_lf, j)
                o = o.reshape(batch, steps, D_MODEL).transpose(1, 0, 2).reshape(m, D_MODEL)
                h = _oproj(o, wts["w_o"][j], h, tm=tm, head_major=False)
        h, ctail = _ffn_ple_layer(
            h, wts["g_ffn"][i], wts["w_u"][i], wts["w_v"][i], wts["conv_w"][i], wts["conv_b"][i],
            wts["w_d"][i], conv_prev[i], wts["g_ple"][i], p[i], wts["w_pe"][i], wts["w_pg"][i],
            wts["g_final"], tm=tm, shift=shift, seq_rows=seq_rows, final_norm=(i == depth - 1))
        conv_tails.append(ctail)
    return h, pool_tails, conv_tails, ks, vs, lfs


def kernel(x_prompt, x_sample, state_pool, state_conv, cache_k, cache_v, cache_lf, page_table,
           p_prompt, p_sample, g_mix, w_pool, pool_scale, w_qkvf, b_f, w_o,
           g_ffn, w_up, conv_w, conv_b, w_down, w_pe, g_ple, w_pg, g_final):
    depth = g_mix.shape[0]
    bp, sp, _ = x_prompt.shape
    bs, ts, _ = x_sample.shape
    d_ff = w_down.shape[1]
    ff_pad = pl.cdiv(d_ff, FF_TILE) * FF_TILE
    padc = ff_pad - d_ff
    n_pool = w_pool.shape[0]
    n_fox = w_o.shape[0]
    assert ts == DEC_T

    wts = {
        "g_mix": g_mix[:, None, :], "g_ffn": g_ffn[:, None, :], "g_ple": g_ple[:, None, :],
        "g_final": g_final[None, :],
        "w_pool": w_pool.astype(BF16), "pool_scale": pool_scale[:, None, :],
        "w_qkv": w_qkvf[:, :, :3 * D_MODEL].astype(BF16),
        "w_f": w_qkvf[:, :, 3 * D_MODEL:].astype(BF16),
        "b_f": b_f[:, None, :],
        "w_o": w_o.astype(BF16),
        "w_u": jnp.pad(w_up[:, :, :d_ff].astype(BF16), ((0, 0), (0, 0), (0, padc))),
        "w_v": jnp.pad(w_up[:, :, d_ff:].astype(BF16), ((0, 0), (0, 0), (0, padc))),
        "conv_w": jnp.pad(conv_w, ((0, 0), (0, 0), (0, padc))),
        "conv_b": jnp.pad(conv_b, ((0, 0), (0, padc)))[:, None, :],
        "w_d": jnp.pad(w_down.astype(BF16), ((0, 0), (0, padc), (0, 0))),
        "w_pe": w_pe.astype(BF16), "w_pg": w_pg.astype(BF16),
    }

    pool_prev_p = jnp.zeros((n_pool, bp, POOL_SPAN, D_MODEL), F32)
    conv_prev_p = jnp.zeros((depth, bp, SUBLANES, ff_pad), F32)
    y_p, pool_p, conv_p, k_p, v_p, lf_p = _trunk(
        x_prompt.reshape(bp * sp, D_MODEL), p_prompt.reshape(depth, bp * sp, -1), wts,
        batch=bp, steps=sp, time_major=False, pos0=0,
        pool_prev=pool_prev_p, conv_prev=conv_prev_p, decode=None)

    n_pages = page_table.shape[1]
    pool_prev_s = jnp.pad(state_pool.transpose(0, 2, 1, 3), ((0, 0), (1, 0), (0, 0), (0, 0)))
    pool_prev_s = pool_prev_s.reshape(n_pool, 1, POOL_SPAN * bs, D_MODEL)
    conv_prev_s = jnp.pad(state_conv.transpose(0, 2, 1, 3), ((0, 0), (0, 0), (0, 0), (0, padc)))
    conv_prev_s = conv_prev_s.reshape(depth, 1, CONV_CTX * bs, ff_pad)
    n_phys = cache_k.shape[1]
    decode = (page_table, cache_k, cache_v,
              cache_lf.reshape(n_fox, n_phys, LF_ROWS, LANES))
    y_s, pool_s, conv_s, k_s, v_s, lf_s = _trunk(
        x_sample.transpose(1, 0, 2).reshape(ts * bs, D_MODEL),
        p_sample.transpose(0, 2, 1, 3).reshape(depth, ts * bs, -1), wts,
        batch=bs, steps=ts, time_major=True, pos0=n_pages * PAGE_SIZE,
        pool_prev=pool_prev_s, conv_prev=conv_prev_s, decode=decode)

    def from_tm(a, lead):
        return a.reshape(lead, bs, -1).transpose(1, 0, 2)

    ctx = POOL_SPAN - 1
    out_pool_p = jnp.stack([t[:, 1:, :] for t in pool_p])
    out_pool_s = jnp.stack([from_tm(t[0], POOL_SPAN)[:, 1:, :] for t in pool_s])
    tps = sp // PROMPT_TILE
    out_conv_p = jnp.stack([t[tps - 1::tps, SUBLANES - CONV_CTX:, :d_ff] for t in conv_p])
    out_conv_s = jnp.stack([from_tm(t[0], CONV_CTX)[:, :, :d_ff] for t in conv_s])
    assert out_pool_p.shape[2] == ctx

    def heads_p(a):
        return a.reshape(bp, sp, N_HEADS, HEAD_DIM)

    def heads_s(a):
        return from_tm(a, ts).reshape(bs, ts, N_HEADS, HEAD_DIM)

    return (y_p.reshape(bp, sp, D_MODEL),
            from_tm(y_s, ts),
            out_pool_p, out_pool_s, out_conv_p, out_conv_s,
            jnp.stack([heads_p(a) for a in k_p]),
            jnp.stack([heads_p(a) for a in v_p]),
            jnp.stack([a.reshape(bp, sp, N_HEADS) for a in lf_p]),
            jnp.stack([heads_s(a) for a in k_s]),
            jnp.stack([heads_s(a) for a in v_s]),
            jnp.stack([from_tm(a, ts) for a in lf_s]))
```

```python
import functools

import jax
import jax.numpy as jnp
from jax import lax
from jax.experimental import pallas as pl
from jax.experimental.pallas import tpu as pltpu

D_MODEL = 2048
N_HEADS = 16
HEAD_DIM = D_MODEL // N_HEADS
POOL_WINDOWS = (2, 4, 8, 16)
POOL_GROUP_DIM = D_MODEL // len(POOL_WINDOWS)
POOL_SPAN = max(POOL_WINDOWS)
CONV_WIDTH = 3
CONV_CTX = CONV_WIDTH - 1
PAGE_SIZE = 128
RMS_EPS = 1e-6
ATTN_SCALE = HEAD_DIM ** -0.5
LOG2E = 1.4426950408889634
Q_SCALE = ATTN_SCALE * LOG2E

LANES = 128
SUBLANES = 8
FF_TILE = 512
VMEM_LIMIT = 56 * 1024 * 1024

F32 = jnp.float32
BF16 = jnp.bfloat16
NEG_INF = float("-inf")


def _params(n_axes):
    return pltpu.CompilerParams(dimension_semantics=("arbitrary",) * n_axes,
                                vmem_limit_bytes=VMEM_LIMIT)


def _rms(x, g):
    return x * lax.rsqrt(jnp.mean(x * x, axis=-1, keepdims=True) + RMS_EPS) * g


def _dot(a, b):
    return jnp.dot(a, b, preferred_element_type=F32)


def _dot_nt(a, b):
    return lax.dot_general(a, b, (((1,), (1,)), ((), ())), preferred_element_type=F32)


def _pool_kernel(h_ref, g_ref, prev_ref, w_ref, sc_ref, o_ref, tail_ref, ext_s,
                 *, tm, shift, tiles_per_seq, pos0):
    i = pl.program_id(0)
    span = POOL_SPAN * shift
    it = i % tiles_per_seq

    @pl.when(it == 0)
    def _():
        ext_s[0:span, :] = prev_ref[...]

    x = h_ref[...]
    xn = _rms(x, g_ref[...])
    ext_s[span:span + tm, :] = xn
    row = lax.broadcasted_iota(jnp.int32, (tm, 1), 0)
    pos = pos0 + (it * tm + row) // shift
    for g, w in enumerate(POOL_WINDOWS):
        cols = slice(g * POOL_GROUP_DIM, (g + 1) * POOL_GROUP_DIM)
        acc = xn[:, cols]
        for k in range(1, w):
            acc = acc + ext_s[span - k * shift:span - k * shift + tm, cols]
        cnt = jnp.minimum(w, pos + 1).astype(F32)
        delta = acc / cnt - xn[:, cols]
        y = _dot(delta.astype(BF16), w_ref[g])
        o_ref[:, cols] = x[:, cols] + y * sc_ref[:, cols]
    tail = ext_s[tm:tm + span, :]
    tail_ref[...] = tail
    if tiles_per_seq > 1:
        ext_s[0:span, :] = tail


def _pool_layer(h, g, prev0, w_pool, scale, *, tm, shift, seq_rows, pos0):
    m = h.shape[0]
    span = POOL_SPAN * shift
    tps = seq_rows // tm
    nseq = m // seq_rows
    assert tps == 1 or tm >= span
    kern = functools.partial(_pool_kernel, tm=tm, shift=shift, tiles_per_seq=tps, pos0=pos0)
    return pl.pallas_call(
        kern,
        grid=(m // tm,),
        in_specs=[
            pl.BlockSpec((tm, D_MODEL), lambda i: (i, 0)),
            pl.BlockSpec((1, D_MODEL), lambda i: (0, 0)),
            pl.BlockSpec((None, span, D_MODEL), lambda i: (i // tps, 0, 0)),
            pl.BlockSpec((len(POOL_WINDOWS), POOL_GROUP_DIM, POOL_GROUP_DIM), lambda i: (0, 0, 0)),
            pl.BlockSpec((1, D_MODEL), lambda i: (0, 0)),
        ],
        out_specs=[
            pl.BlockSpec((tm, D_MODEL), lambda i: (i, 0)),
            pl.BlockSpec((None, span, D_MODEL), lambda i: (i // tps, 0, 0)),
        ],
        out_shape=[jax.ShapeDtypeStruct((m, D_MODEL), F32),
                   jax.ShapeDtypeStruct((nseq, span, D_MODEL), F32)],
        scratch_shapes=[pltpu.VMEM((span + tm, D_MODEL), F32)],
        compiler_params=_params(1),
        name="pool_layer",
    )(h, g, prev0, w_pool, scale)


QKV_TILE = 512
HEADS_PER_TILE = QKV_TILE // HEAD_DIM
TILES_PER_PROJ = D_MODEL // QKV_TILE


def _qkvf_kernel(h_ref, g_ref, w_ref, wf_ref, bf_ref,
                 qh_ref, kh_ref, vh_ref, k_ref, v_ref, lf_ref, xn_s, *, transpose_qv):
    j = pl.program_id(1)

    @pl.when(j == 0)
    def _():
        xn = _rms(h_ref[...], g_ref[...]).astype(BF16)
        xn_s[...] = xn
        z = _dot(xn, wf_ref[...]) + bf_ref[...]
        lf_ref[...] = jnp.minimum(z, 0.0) - jnp.log1p(jnp.exp(-jnp.abs(z)))

    res = _dot(xn_s[...], w_ref[...])

    def heads(dst, scale=None, transpose=False):
        for hh in range(HEADS_PER_TILE):
            x = res[:, hh * HEAD_DIM:(hh + 1) * HEAD_DIM]
            if scale is not None:
                x = x * scale
            dst[hh] = (x.T if transpose else x).astype(BF16)

    @pl.when(j < TILES_PER_PROJ)
    def _():
        heads(qh_ref, Q_SCALE, transpose_qv)

    @pl.when((j >= TILES_PER_PROJ) & (j < 2 * TILES_PER_PROJ))
    def _():
        heads(kh_ref)
        k_ref[...] = res

    @pl.when(j >= 2 * TILES_PER_PROJ)
    def _():
        heads(vh_ref, None, transpose_qv)
        v_ref[...] = res


def _qkvf(h, g, w_qkv, w_f, b_f, *, tm, transpose_qv):
    m = h.shape[0]
    tpp = TILES_PER_PROJ

    def sel(which):
        return lambda i, j: jnp.clip(j - which * tpp, 0, tpp - 1)

    hm_shape = jax.ShapeDtypeStruct((N_HEADS, m, HEAD_DIM), BF16)
    hm_t_shape = jax.ShapeDtypeStruct((N_HEADS, HEAD_DIM, m), BF16)

    def hm_t_spec(which):
        s = sel(which)
        return pl.BlockSpec((HEADS_PER_TILE, HEAD_DIM, tm), lambda i, j: (s(i, j), 0, i))
    flat_shape = jax.ShapeDtypeStruct((m, D_MODEL), F32)

    def hm_spec(which):
        s = sel(which)
        return pl.BlockSpec((HEADS_PER_TILE, tm, HEAD_DIM), lambda i, j: (s(i, j), i, 0))

    def flat_spec(which):
        s = sel(which)
        return pl.BlockSpec((tm, QKV_TILE), lambda i, j: (i, s(i, j)))

    qv_spec = hm_t_spec if transpose_qv else hm_spec
    qv_shape = hm_t_shape if transpose_qv else hm_shape
    return pl.pallas_call(
        functools.partial(_qkvf_kernel, transpose_qv=transpose_qv),
        grid=(m // tm, 3 * tpp),
        in_specs=[
            pl.BlockSpec((tm, D_MODEL), lambda i, j: (i, 0)),
            pl.BlockSpec((1, D_MODEL), lambda i, j: (0, 0)),
            pl.BlockSpec((D_MODEL, QKV_TILE), lambda i, j: (0, j)),
            pl.BlockSpec((D_MODEL, N_HEADS), lambda i, j: (0, 0)),
            pl.BlockSpec((1, N_HEADS), lambda i, j: (0, 0)),
        ],
        out_specs=[qv_spec(0), hm_spec(1), qv_spec(2), flat_spec(1), flat_spec(2),
                   pl.BlockSpec((tm, N_HEADS), lambda i, j: (i, 0))],
        out_shape=[qv_shape, hm_shape, qv_shape, flat_shape, flat_shape,
                   jax.ShapeDtypeStruct((m, N_HEADS), F32)],
        scratch_shapes=[pltpu.VMEM((tm, D_MODEL), BF16)],
        compiler_params=_params(2),
        name="fox_qkvf",
    )(h, g, w_qkv, w_f, b_f)


def _lane_cumsum(x):
    n = x.shape[-1]
    lane = lax.broadcasted_iota(jnp.int32, x.shape, x.ndim - 1)
    k = 1
    while k < n:
        x = x + jnp.where(lane >= k, pltpu.roll(x, k, x.ndim - 1), 0.0)
        k *= 2
    return x


def _cumsum_kernel(x_ref, o_ref):
    o_ref[...] = _lane_cumsum(x_ref[...]) * LOG2E


def _cumsum_last(x):
    b, hh, s = x.shape
    return pl.pallas_call(
        _cumsum_kernel,
        grid=(b,),
        in_specs=[pl.BlockSpec((None, hh, s), lambda i: (i, 0, 0))],
        out_specs=pl.BlockSpec((None, hh, s), lambda i: (i, 0, 0)),
        out_shape=jax.ShapeDtypeStruct(x.shape, F32),
        compiler_params=_params(1),
        name="forget_cumsum",
    )(x)


AUG_DIM = 2 * HEAD_DIM
N_SPLIT = 3


def _split_bf16(x):
    parts = []
    for _ in range(N_SPLIT):
        piece = x.astype(BF16).astype(F32)
        parts.append(piece)
        x = x - piece
    return parts


def _augment_kernel(qt_ref, kh_ref, c_ref, ct_ref, qa_ref, ka_ref, *, tm):
    sub = lax.broadcasted_iota(jnp.int32, (HEAD_DIM, tm), 0)
    lane = lax.broadcasted_iota(jnp.int32, (tm, HEAD_DIM), 1)
    c = c_ref[...]
    ct = ct_ref[...]
    for h in range(N_HEADS):
        q_parts = _split_bf16(ct[h:h + 1, :])
        aug_q = jnp.where(sub < 2 * N_SPLIT, 1.0, 0.0)
        for n in range(N_SPLIT):
            aug_q = jnp.where(sub == n, q_parts[n], aug_q)
        qa_ref[h, 0:HEAD_DIM, :] = qt_ref[h]
        qa_ref[h, HEAD_DIM:AUG_DIM, :] = aug_q.astype(BF16)
        k_parts = _split_bf16(c[:, h:h + 1])
        aug_k = jnp.where(lane < N_SPLIT, 1.0, 0.0)
        for n in range(N_SPLIT):
            aug_k = jnp.where(lane == N_SPLIT + n, -k_parts[n], aug_k)
        ka_ref[h, :, 0:HEAD_DIM] = kh_ref[h]
        ka_ref[h, :, HEAD_DIM:AUG_DIM] = aug_k.astype(BF16)


def _augment(qt, kh, c, ct, *, tm):
    m = kh.shape[1]
    per_seq = ct.shape[2] // tm
    return pl.pallas_call(
        functools.partial(_augment_kernel, tm=tm),
        grid=(m // tm,),
        in_specs=[
            pl.BlockSpec((N_HEADS, HEAD_DIM, tm), lambda i: (0, 0, i)),
            pl.BlockSpec((N_HEADS, tm, HEAD_DIM), lambda i: (0, i, 0)),
            pl.BlockSpec((tm, N_HEADS), lambda i: (i, 0)),
            pl.BlockSpec((None, N_HEADS, tm), lambda i: (i // per_seq, 0, i % per_seq)),
        ],
        out_specs=[
            pl.BlockSpec((N_HEADS, AUG_DIM, tm), lambda i: (0, 0, i)),
            pl.BlockSpec((N_HEADS, tm, AUG_DIM), lambda i: (0, i, 0)),
        ],
        out_shape=[jax.ShapeDtypeStruct((N_HEADS, AUG_DIM, m), BF16),
                   jax.ShapeDtypeStruct((N_HEADS, m, AUG_DIM), BF16)],
        compiler_params=_params(1),
        name="fox_bias_augment",
    )(qt, kh, c, ct)


def _flash_kernel(q_ref, k_ref, v_ref, o_ref, m_s, l_s, acc_s, *, tq):
    qi = pl.program_id(1)
    ki = pl.program_id(2)

    @pl.when(ki == 0)
    def _():
        m_s[...] = jnp.full(m_s.shape, NEG_INF, F32)
        l_s[...] = jnp.zeros(l_s.shape, F32)
        acc_s[...] = jnp.zeros(acc_s.shape, F32)

    def step(diagonal):
        def body(h, carry):
            s = _dot(k_ref[h], q_ref[h])
            if diagonal:
                key = lax.broadcasted_iota(jnp.int32, (tq, tq), 0)
                qry = lax.broadcasted_iota(jnp.int32, (tq, tq), 1)
                s = jnp.where(key <= qry, s, NEG_INF)
            m_prev = m_s[h]
            m_new = jnp.maximum(m_prev, jnp.max(s, axis=0, keepdims=True))
            alpha = jnp.exp2(m_prev - m_new)
            p = jnp.exp2(s - m_new)
            l_s[h] = alpha * l_s[h] + jnp.sum(p, axis=0, keepdims=True)
            acc_s[h] = alpha * acc_s[h] + _dot(v_ref[h], p.astype(BF16))
            m_s[h] = m_new
            return carry

        lax.fori_loop(0, N_HEADS, body, 0, unroll=4)

    @pl.when(ki < qi)
    def _():
        step(False)

    @pl.when(ki == qi)
    def _():
        step(True)

        def fin(h, carry):
            o_ref[h] = (acc_s[h] / l_s[h]).T.astype(BF16)
            return carry

        lax.fori_loop(0, N_HEADS, fin, 0)


def _flash_prompt(q_aug, k_aug, vt, *, batch, seq, tq):
    nq = seq // tq
    m = k_aug.shape[1]

    def kv_tile(b, qi, ki):
        return b * nq + jnp.minimum(ki, qi)

    row = pltpu.VMEM((N_HEADS, 1, tq), F32)
    return pl.pallas_call(
        functools.partial(_flash_kernel, tq=tq),
        grid=(batch, nq, nq),
        in_specs=[
            pl.BlockSpec((N_HEADS, AUG_DIM, tq), lambda b, qi, ki: (0, 0, b * nq + qi)),
            pl.BlockSpec((N_HEADS, tq, AUG_DIM), lambda b, qi, ki: (0, kv_tile(b, qi, ki), 0)),
            pl.BlockSpec((N_HEADS, HEAD_DIM, tq), lambda b, qi, ki: (0, 0, kv_tile(b, qi, ki))),
        ],
        out_specs=pl.BlockSpec((N_HEADS, tq, HEAD_DIM), lambda b, qi, ki: (0, b * nq + qi, 0)),
        out_shape=jax.ShapeDtypeStruct((N_HEADS, m, HEAD_DIM), BF16),
        scratch_shapes=[row, row, pltpu.VMEM((N_HEADS, HEAD_DIM, tq), F32)],
        compiler_params=_params(3),
        name="fox_prompt_attention",
    )(q_aug, k_aug, vt)


DEC_T = 4
DEC_ROWS = DEC_T * N_HEADS
PAGE_ROWS = PAGE_SIZE * N_HEADS
LF_ROWS = PAGE_ROWS // LANES
PAGES_PER_STEP = 4


def _row_to_col(r):
    n = r.shape[-1]
    eye = (lax.broadcasted_iota(jnp.int32, (n, n), 0) == lax.broadcasted_iota(jnp.int32, (n, n), 1))
    return jnp.sum(jnp.where(eye, jnp.broadcast_to(r, (n, n)), 0.0), axis=1, keepdims=True)


def _strided_prefix(x, lane, steps):
    for k in steps:
        x = x + jnp.where(lane >= k, pltpu.roll(x, k, 1), 0.0)
    return x


def _decode_kernel(pt_ref, q_ref, kn_ref, vn_ref, lfn_ref, *rest, pps):
    k_refs, v_refs, lf_refs = rest[:pps], rest[pps:2 * pps], rest[2 * pps:3 * pps]
    o_ref, m_s, l_s, acc_s, carry_s, mask_s = rest[3 * pps:]
    pg = pl.program_id(1)
    lane = lax.broadcasted_iota(jnp.int32, (1, LANES), 1)
    q = q_ref[...]

    @pl.when(pg == 0)
    def _():
        cn_row = _strided_prefix(lfn_ref[...] * LOG2E, lane, (N_HEADS, 2 * N_HEADS))
        cn_col = _row_to_col(cn_row)[:DEC_ROWS]
        row_h = lax.broadcasted_iota(jnp.int32, (DEC_ROWS, PAGE_ROWS), 0) % N_HEADS
        col_h = lax.broadcasted_iota(jnp.int32, (DEC_ROWS, PAGE_ROWS), 1) % N_HEADS
        mask_s[...] = jnp.where(row_h == col_h, 0.0, NEG_INF) + cn_col
        s = _dot_nt(q, kn_ref[...])
        r = lax.broadcasted_iota(jnp.int32, (DEC_ROWS, LANES), 0)
        c = lax.broadcasted_iota(jnp.int32, (DEC_ROWS, LANES), 1)
        valid = (c < DEC_ROWS) & (c % N_HEADS == r % N_HEADS) & (c // N_HEADS <= r // N_HEADS)
        s = jnp.where(valid, s + (cn_col - cn_row), NEG_INF)
        m0 = jnp.max(s, axis=-1, keepdims=True)
        p0 = jnp.exp2(s - m0)
        m_s[...] = m0
        l_s[...] = jnp.sum(p0, axis=-1, keepdims=True)
        acc_s[...] = _dot(p0.astype(BF16), vn_ref[...])
        carry_s[...] = jnp.zeros(carry_s.shape, F32)

    sub = lax.broadcasted_iota(jnp.int32, (LF_ROWS, LANES), 0)
    lane_steps = (N_HEADS, 2 * N_HEADS, 4 * N_HEADS)
    carry = carry_s[...]
    logits = []
    for u in range(pps):
        x = _strided_prefix(lf_refs[u][...] * LOG2E, lane, lane_steps)
        y = jnp.where(lane >= LANES - N_HEADS, x, 0.0)
        for k in lane_steps:
            y = y + pltpu.roll(y, k, 1)
        e = y
        k = 1
        while k < LF_ROWS:
            e = e + jnp.where(sub >= k, pltpu.roll(e, k, 0), 0.0)
            k *= 2
        tot = e[LF_ROWS - 1:LF_ROWS, :]
        suffix = (tot - (x + (e - y))) + carry
        carry = carry + tot
        bias = jnp.concatenate([suffix[r:r + 1, :] for r in range(LF_ROWS)], axis=1)
        kp = k_refs[u][...].reshape(PAGE_ROWS, HEAD_DIM).astype(BF16)
        logits.append(_dot_nt(q, kp) + bias + mask_s[...])
    carry_s[...] = carry

    m_prev = m_s[...]
    m_new = m_prev
    for s in logits:
        m_new = jnp.maximum(m_new, jnp.max(s, axis=-1, keepdims=True))
    alpha = jnp.exp2(m_prev - m_new)
    l_new = alpha * l_s[...]
    acc = alpha * acc_s[...]
    for u in range(pps):
        p = jnp.exp2(logits[u] - m_new)
        l_new = l_new + jnp.sum(p, axis=-1, keepdims=True)
        vp = v_refs[u][...].reshape(PAGE_ROWS, HEAD_DIM).astype(BF16)
        acc = acc + _dot(p.astype(BF16), vp)
    m_s[...] = m_new
    l_s[...] = l_new
    acc_s[...] = acc

    @pl.when(pg == pl.num_programs(1) - 1)
    def _():
        o_ref[...] = acc / l_new


def _decode_attention(page_table, q, k_new, v_new, lf_new, cache_k, cache_v, cache_lf, layer):
    nb, n_pages = page_table.shape
    pps = PAGES_PER_STEP
    assert n_pages % pps == 0

    def new_map(b, pg, pt):
        return (b, 0, 0)

    def page_spec(block, u):
        zeros = (0,) * (len(block) - 2)
        return pl.BlockSpec(
            block, lambda b, pg, pt: (layer, pt[b, n_pages - 1 - (pg * pps + u)]) + zeros)

    kv_block = (None, None, PAGE_SIZE, N_HEADS, HEAD_DIM)
    lf_block = (None, None, LF_ROWS, LANES)
    col = pltpu.VMEM((DEC_ROWS, 1), F32)
    grid_spec = pltpu.PrefetchScalarGridSpec(
        num_scalar_prefetch=1,
        grid=(nb, n_pages // pps),
        in_specs=[
            pl.BlockSpec((None, DEC_ROWS, HEAD_DIM), new_map),
            pl.BlockSpec((None, LANES, HEAD_DIM), new_map),
            pl.BlockSpec((None, LANES, HEAD_DIM), new_map),
            pl.BlockSpec((None, 1, LANES), new_map),
            *[page_spec(kv_block, u) for u in range(pps)],
            *[page_spec(kv_block, u) for u in range(pps)],
            *[page_spec(lf_block, u) for u in range(pps)],
        ],
        out_specs=pl.BlockSpec((None, DEC_ROWS, HEAD_DIM), new_map),
        scratch_shapes=[col, col, pltpu.VMEM((DEC_ROWS, HEAD_DIM), F32),
                        pltpu.VMEM((1, LANES), F32),
                        pltpu.VMEM((DEC_ROWS, PAGE_ROWS), F32)],
    )
    return pl.pallas_call(
        functools.partial(_decode_kernel, pps=pps),
        grid_spec=grid_spec,
        out_shape=jax.ShapeDtypeStruct((nb, DEC_ROWS, HEAD_DIM), F32),
        compiler_params=_params(2),
        name="fox_sample_attention",
    )(page_table, q, k_new, v_new, lf_new,
      *([cache_k] * pps), *([cache_v] * pps), *([cache_lf] * pps))


OUT_TILE = 512


def _oproj_kernel(o_ref, w_ref, h_ref, out_ref, *, head_major):
    if head_major:
        o = jnp.concatenate([o_ref[h] for h in range(N_HEADS)], axis=-1)
    else:
        o = o_ref[...].astype(BF16)
    out_ref[...] = h_ref[...] + _dot(o, w_ref[...])


def _oproj(o, w, h, *, tm, head_major):
    m = h.shape[0]
    if head_major:
        o_spec = pl.BlockSpec((N_HEADS, tm, HEAD_DIM), lambda i, j: (0, i, 0))
    else:
        o_spec = pl.BlockSpec((tm, D_MODEL), lambda i, j: (i, 0))
    return pl.pallas_call(
        functools.partial(_oproj_kernel, head_major=head_major),
        grid=(m // tm, D_MODEL // OUT_TILE),
        in_specs=[o_spec,
                  pl.BlockSpec((D_MODEL, OUT_TILE), lambda i, j: (0, j)),
                  pl.BlockSpec((tm, OUT_TILE), lambda i, j: (i, j))],
        out_specs=pl.BlockSpec((tm, OUT_TILE), lambda i, j: (i, j)),
        out_shape=jax.ShapeDtypeStruct((m, D_MODEL), F32),
        compiler_params=_params(2),
        name="fox_out_proj",
    )(o, w, h)


N_OUT_TILES = D_MODEL // OUT_TILE


def _ffn_ple_kernel(h_ref, g_ref, wu_ref, wv_ref, cw_ref, cb_ref, wd_ref, prev_ref,
                    gp_ref, p_ref, wpe_ref, wpg_ref, gf_ref,
                    out_ref, tail_ref, xn_s, p_s, ue_s, carry_s,
                    *, tm, shift, keep, tiles_per_seq, nj, final_norm):
    i = pl.program_id(0)
    j = pl.program_id(1)

    @pl.when(j == 0)
    def _():
        xn_s[...] = _rms(h_ref[...], g_ref[...]).astype(BF16)
        out_ref[...] = h_ref[...]

    @pl.when(j < nj)
    def _():
        @pl.when(i % tiles_per_seq == 0)
        def _():
            carry_s[j] = prev_ref[...]

        xn = xn_s[...]
        u = _dot(xn, wu_ref[...])
        v = _dot(xn, wv_ref[...])
        ue_s[0:keep, :] = carry_s[j]
        ue_s[keep:keep + tm, :] = u
        cw = cw_ref[...]
        c = cb_ref[...]
        c = c + cw[0:1, :] * ue_s[keep - 2 * shift:keep - 2 * shift + tm, :]
        c = c + cw[1:2, :] * ue_s[keep - shift:keep - shift + tm, :]
        c = c + cw[2:3, :] * u
        gelu = 0.5 * c * (1.0 + jnp.tanh(0.7978845608028654 * (c + 0.044715 * (c * c * c))))
        out_ref[...] += _dot((gelu * v).astype(BF16), wd_ref[...])
        tail = ue_s[tm:tm + keep, :]
        carry_s[j] = tail
        tail_ref[...] = tail

    @pl.when(j == nj - 1)
    def _():
        xn_s[...] = _rms(out_ref[...], gp_ref[...]).astype(BF16)
        p_s[...] = p_ref[...].astype(BF16)

    for n in range(N_OUT_TILES):
        @pl.when(j == nj + n)
        def _(n=n):
            cols = slice(n * OUT_TILE, (n + 1) * OUT_TILE)
            e = _dot(p_s[...], wpe_ref[...])
            z = _dot(xn_s[...], wpg_ref[...])
            out_ref[:, cols] = out_ref[:, cols] + e * jax.nn.sigmoid(z)
            if final_norm and n == N_OUT_TILES - 1:
                out_ref[...] = _rms(out_ref[...], gf_ref[...])


def _ffn_ple_layer(h, g, w_u, w_v, conv_w, conv_b, w_d, prev0, g_ple, p, w_pe, w_pg, g_final,
                   *, tm, shift, seq_rows, final_norm):
    m = h.shape[0]
    ff = w_u.shape[1]
    pd = p.shape[1]
    nj = ff // FF_TILE
    keep = prev0.shape[1]
    tps = seq_rows // tm
    assert keep >= CONV_CTX * shift and tm >= keep and keep % SUBLANES == 0
    kern = functools.partial(_ffn_ple_kernel, tm=tm, shift=shift, keep=keep, tiles_per_seq=tps,
                             nj=nj, final_norm=final_norm)

    def ffj(j):
        return jnp.minimum(j, nj - 1)

    def plej(j):
        return jnp.clip(j - nj, 0, N_OUT_TILES - 1)

    row = pl.BlockSpec((1, D_MODEL), lambda i, j: (0, 0))
    return pl.pallas_call(
        kern,
        grid=(m // tm, nj + N_OUT_TILES),
        in_specs=[
            pl.BlockSpec((tm, D_MODEL), lambda i, j: (i, 0)),
            row,
            pl.BlockSpec((D_MODEL, FF_TILE), lambda i, j: (0, ffj(j))),
            pl.BlockSpec((D_MODEL, FF_TILE), lambda i, j: (0, ffj(j))),
            pl.BlockSpec((CONV_WIDTH, FF_TILE), lambda i, j: (0, ffj(j))),
            pl.BlockSpec((1, FF_TILE), lambda i, j: (0, ffj(j))),
            pl.BlockSpec((FF_TILE, D_MODEL), lambda i, j: (ffj(j), 0)),
            pl.BlockSpec((None, keep, FF_TILE), lambda i, j: (i // tps, 0, ffj(j))),
            row,
            pl.BlockSpec((tm, pd), lambda i, j: (i, 0)),
            pl.BlockSpec((pd, OUT_TILE), lambda i, j: (0, plej(j))),
            pl.BlockSpec((D_MODEL, OUT_TILE), lambda i, j: (0, plej(j))),
            row,
        ],
        out_specs=[
            pl.BlockSpec((tm, D_MODEL), lambda i, j: (i, 0)),
            pl.BlockSpec((None, keep, FF_TILE), lambda i, j: (i, 0, ffj(j))),
        ],
        out_shape=[jax.ShapeDtypeStruct((m, D_MODEL), F32),
                   jax.ShapeDtypeStruct((m // tm, keep, ff), F32)],
        scratch_shapes=[pltpu.VMEM((tm, D_MODEL), BF16),
                        pltpu.VMEM((tm, pd), BF16),
                        pltpu.VMEM((keep + tm, FF_TILE), F32),
                        pltpu.VMEM((nj, keep, FF_TILE), F32)],
        compiler_params=_params(2),
        name="conv_ffn_ple",
    )(h, g, w_u, w_v, conv_w, conv_b, w_d, prev0, g_ple, p, w_pe, w_pg, g_final)


PROMPT_TILE = 512
QKV_ROWS = 1024
FLASH_TILE = 512


def _trunk(h, p, wts, *, batch, steps, time_major, pos0, pool_prev, conv_prev, decode):
    depth = p.shape[0]
    m = h.shape[0]
    if time_major:
        tm, tm_qkv, shift, seq_rows = m, m, batch, m
    else:
        tm, tm_qkv, shift, seq_rows = PROMPT_TILE, QKV_ROWS, 1, steps
    pool_tails, conv_tails, ks, vs, lfs = [], [], [], [], []
    for i in range(depth):
        j = i // 2
        if i % 2 == 0:
            h, tail = _pool_layer(h, wts["g_mix"][i], pool_prev[j], wts["w_pool"][j],
                                  wts["pool_scale"][j], tm=tm, shift=shift,
                                  seq_rows=seq_rows, pos0=pos0)
            pool_tails.append(tail)
        else:
            qh, kh, vh, k, v, lf = _qkvf(h, wts["g_mix"][i], wts["w_qkv"][j], wts["w_f"][j],
                                         wts["b_f"][j], tm=tm_qkv, transpose_qv=decode is None)
            ks.append(k)
            vs.append(v)
            lfs.append(lf)
            if decode is None:
                lf_t = lf.reshape(batch, steps, N_HEADS).transpose(0, 2, 1)
                ct = _cumsum_last(lf_t)
                c = ct.transpose(0, 2, 1).reshape(m, N_HEADS)
                q_aug, k_aug = _augment(qh, kh, c, ct, tm=FLASH_TILE)
                o = _flash_prompt(q_aug, k_aug, vh, batch=batch, seq=steps, tq=FLASH_TILE)
                h = _oproj(o, wts["w_o"][j], h, tm=tm, head_major=True)
            else:
                page_table, cache_k, cache_v, cache_lf = decode

                def per_seq(a):
                    a = a.reshape(steps, batch, N_HEADS, -1).transpose(1, 0, 2, 3)
                    return a.reshape(batch, steps * N_HEADS, -1)

                def as_page(a):
                    return jnp.pad(a, ((0, 0), (0, LANES - DEC_ROWS), (0, 0))).astype(BF16)

                q_rows = per_seq(qh.transpose(1, 0, 2).reshape(m, D_MODEL))
                lf_rows = per_seq(lf).reshape(batch, 1, DEC_ROWS)
                lf_rows = jnp.pad(lf_rows, ((0, 0), (0, 0), (0, LANES - DEC_ROWS)))
                o = _decode_attention(page_table, q_rows, as_page(per_seq(k)), as_page(per_seq(v)),
                                      lf_rows, cache_k, cache_v, cache_lf, j)
                o = o.reshape(batch, steps, D_MODEL).transpose(1, 0, 2).reshape(m, D_MODEL)
                h = _oproj(o, wts["w_o"][j], h, tm=tm, head_major=False)
        h, ctail = _ffn_ple_layer(
            h, wts["g_ffn"][i], wts["w_u"][i], wts["w_v"][i], wts["conv_w"][i], wts["conv_b"][i],
            wts["w_d"][i], conv_prev[i], wts["g_ple"][i], p[i], wts["w_pe"][i], wts["w_pg"][i],
            wts["g_final"], tm=tm, shift=shift, seq_rows=seq_rows, final_norm=(i == depth - 1))
        conv_tails.append(ctail)
    return h, pool_tails, conv_tails, ks, vs, lfs


def kernel(x_prompt, x_sample, state_pool, state_conv, cache_k, cache_v, cache_lf, page_table,
           p_prompt, p_sample, g_mix, w_pool, pool_scale, w_qkvf, b_f, w_o,
           g_ffn, w_up, conv_w, conv_b, w_down, w_pe, g_ple, w_pg, g_final):
    depth = g_mix.shape[0]
    bp, sp, _ = x_prompt.shape
    bs, ts, _ = x_sample.shape
    d_ff = w_down.shape[1]
    ff_pad = pl.cdiv(d_ff, FF_TILE) * FF_TILE
    padc = ff_pad - d_ff
    n_pool = w_pool.shape[0]
    n_fox = w_o.shape[0]
    assert ts == DEC_T

    wts = {
        "g_mix": g_mix[:, None, :], "g_ffn": g_ffn[:, None, :], "g_ple": g_ple[:, None, :],
        "g_final": g_final[None, :],
        "w_pool": w_pool.astype(BF16), "pool_scale": pool_scale[:, None, :],
        "w_qkv": w_qkvf[:, :, :3 * D_MODEL].astype(BF16),
        "w_f": w_qkvf[:, :, 3 * D_MODEL:].astype(BF16),
        "b_f": b_f[:, None, :],
        "w_o": w_o.astype(BF16),
        "w_u": jnp.pad(w_up[:, :, :d_ff].astype(BF16), ((0, 0), (0, 0), (0, padc))),
        "w_v": jnp.pad(w_up[:, :, d_ff:].astype(BF16), ((0, 0), (0, 0), (0, padc))),
        "conv_w": jnp.pad(conv_w, ((0, 0), (0, 0), (0, padc))),
        "conv_b": jnp.pad(conv_b, ((0, 0), (0, padc)))[:, None, :],
        "w_d": jnp.pad(w_down.astype(BF16), ((0, 0), (0, padc), (0, 0))),
        "w_pe": w_pe.astype(BF16), "w_pg": w_pg.astype(BF16),
    }

    pool_prev_p = jnp.zeros((n_pool, bp, POOL_SPAN, D_MODEL), F32)
    conv_prev_p = jnp.zeros((depth, bp, SUBLANES, ff_pad), F32)
    y_p, pool_p, conv_p, k_p, v_p, lf_p = _trunk(
        x_prompt.reshape(bp * sp, D_MODEL), p_prompt.reshape(depth, bp * sp, -1), wts,
        batch=bp, steps=sp, time_major=False, pos0=0,
        pool_prev=pool_prev_p, conv_prev=conv_prev_p, decode=None)

    n_pages = page_table.shape[1]
    pool_prev_s = jnp.pad(state_pool.transpose(0, 2, 1, 3), ((0, 0), (1, 0), (0, 0), (0, 0)))
    pool_prev_s = pool_prev_s.reshape(n_pool, 1, POOL_SPAN * bs, D_MODEL)
    conv_prev_s = jnp.pad(state_conv.transpose(0, 2, 1, 3), ((0, 0), (0, 0), (0, 0), (0, padc)))
    conv_prev_s = conv_prev_s.reshape(depth, 1, CONV_CTX * bs, ff_pad)
    n_phys = cache_k.shape[1]
    decode = (page_table, cache_k, cache_v,
              cache_lf.reshape(n_fox, n_phys, LF_ROWS, LANES))
    y_s, pool_s, conv_s, k_s, v_s, lf_s = _trunk(
        x_sample.transpose(1, 0, 2).reshape(ts * bs, D_MODEL),
        p_sample.transpose(0, 2, 1, 3).reshape(depth, ts * bs, -1), wts,
        batch=bs, steps=ts, time_major=True, pos0=n_pages * PAGE_SIZE,
        pool_prev=pool_prev_s, conv_prev=conv_prev_s, decode=decode)

    def from_tm(a, lead):
        return a.reshape(lead, bs, -1).transpose(1, 0, 2)

    ctx = POOL_SPAN - 1
    out_pool_p = jnp.stack([t[:, 1:, :] for t in pool_p])
    out_pool_s = jnp.stack([from_tm(t[0], POOL_SPAN)[:, 1:, :] for t in pool_s])
    tps = sp // PROMPT_TILE
    out_conv_p = jnp.stack([t[tps - 1::tps, SUBLANES - CONV_CTX:, :d_ff] for t in conv_p])
    out_conv_s = jnp.stack([from_tm(t[0], CONV_CTX)[:, :, :d_ff] for t in conv_s])
    assert out_pool_p.shape[2] == ctx

    def heads_p(a):
        return a.reshape(bp, sp, N_HEADS, HEAD_DIM)

    def heads_s(a):
        return from_tm(a, ts).reshape(bs, ts, N_HEADS, HEAD_DIM)

    return (y_p.reshape(bp, sp, D_MODEL),
            from_tm(y_s, ts),
            out_pool_p, out_pool_s, out_conv_p, out_conv_s,
            jnp.stack([heads_p(a) for a in k_p]),
            jnp.stack([heads_p(a) for a in v_p]),
            jnp.stack([a.reshape(bp, sp, N_HEADS) for a in lf_p]),
            jnp.stack([heads_s(a) for a in k_s]),
            jnp.stack([heads_s(a) for a in v_s]),
            jnp.stack([from_tm(a, ts) for a in lf_s]))
```

```python
import functools

import jax
import jax.numpy as jnp
from jax import lax
from jax.experimental import pallas as pl
from jax.experimental.pallas import tpu as pltpu

D_MODEL = 2048
N_HEADS = 16
HEAD_DIM = D_MODEL // N_HEADS
POOL_WINDOWS = (2, 4, 8, 16)
POOL_GROUP_DIM = D_MODEL // len(POOL_WINDOWS)
POOL_SPAN = max(POOL_WINDOWS)
CONV_WIDTH = 3
CONV_CTX = CONV_WIDTH - 1
PAGE_SIZE = 128
RMS_EPS = 1e-6
ATTN_SCALE = HEAD_DIM ** -0.5
LOG2E = 1.4426950408889634
Q_SCALE = ATTN_SCALE * LOG2E

LANES = 128
SUBLANES = 8
FF_TILE = 512
VMEM_LIMIT = 56 * 1024 * 1024

F32 = jnp.float32
BF16 = jnp.bfloat16
NEG_INF = float("-inf")


def _params(n_axes):
    return pltpu.CompilerParams(dimension_semantics=("arbitrary",) * n_axes,
                                vmem_limit_bytes=VMEM_LIMIT)


def _rms(x, g):
    return x * lax.rsqrt(jnp.mean(x * x, axis=-1, keepdims=True) + RMS_EPS) * g


def _dot(a, b):
    return jnp.dot(a, b, preferred_element_type=F32)


def _dot_nt(a, b):
    return lax.dot_general(a, b, (((1,), (1,)), ((), ())), preferred_element_type=F32)


def _pool_kernel(h_ref, g_ref, prev_ref, w_ref, sc_ref, o_ref, tail_ref, ext_s,
                 *, tm, shift, tiles_per_seq, pos0):
    i = pl.program_id(0)
    span = POOL_SPAN * shift
    it = i % tiles_per_seq

    @pl.when(it == 0)
    def _():
        ext_s[0:span, :] = prev_ref[...]

    x = h_ref[...]
    xn = _rms(x, g_ref[...])
    ext_s[span:span + tm, :] = xn
    row = lax.broadcasted_iota(jnp.int32, (tm, 1), 0)
    pos = pos0 + (it * tm + row) // shift
    for g, w in enumerate(POOL_WINDOWS):
        cols = slice(g * POOL_GROUP_DIM, (g + 1) * POOL_GROUP_DIM)
        acc = xn[:, cols]
        for k in range(1, w):
            acc = acc + ext_s[span - k * shift:span - k * shift + tm, cols]
        cnt = jnp.minimum(w, pos + 1).astype(F32)
        delta = acc / cnt - xn[:, cols]
        y = _dot(delta.astype(BF16), w_ref[g])
        o_ref[:, cols] = x[:, cols] + y * sc_ref[:, cols]
    tail = ext_s[tm:tm + span, :]
    tail_ref[...] = tail
    if tiles_per_seq > 1:
        ext_s[0:span, :] = tail


def _pool_layer(h, g, prev0, w_pool, scale, *, layer, pool_idx, tm, shift, seq_rows, pos0):
    m = h.shape[0]
    span = POOL_SPAN * shift
    tps = seq_rows // tm
    nseq = m // seq_rows
    assert tps == 1 or tm >= span
    kern = functools.partial(_pool_kernel, tm=tm, shift=shift, tiles_per_seq=tps, pos0=pos0)
    return pl.pallas_call(
        kern,
        grid=(m // tm,),
        in_specs=[
            pl.BlockSpec((tm, D_MODEL), lambda i: (i, 0)),
            pl.BlockSpec((None, 1, D_MODEL), lambda i: (layer, 0, 0)),
            pl.BlockSpec((None, None, span, D_MODEL), lambda i: (pool_idx, i // tps, 0, 0)),
            pl.BlockSpec((None, len(POOL_WINDOWS), POOL_GROUP_DIM, POOL_GROUP_DIM),
                         lambda i: (pool_idx, 0, 0, 0)),
            pl.BlockSpec((None, 1, D_MODEL), lambda i: (pool_idx, 0, 0)),
        ],
        out_specs=[
            pl.BlockSpec((tm, D_MODEL), lambda i: (i, 0)),
            pl.BlockSpec((None, span, D_MODEL), lambda i: (i // tps, 0, 0)),
        ],
        out_shape=[jax.ShapeDtypeStruct((m, D_MODEL), F32),
                   jax.ShapeDtypeStruct((nseq, span, D_MODEL), F32)],
        scratch_shapes=[pltpu.VMEM((span + tm, D_MODEL), F32)],
        compiler_params=_params(1),
        name="pool_layer",
    )(h, g, prev0, w_pool, scale)


QKV_TILE = 512
HEADS_PER_TILE = QKV_TILE // HEAD_DIM
TILES_PER_PROJ = D_MODEL // QKV_TILE


def _qkvf_kernel(h_ref, g_ref, w_ref, wf_ref, bf_ref,
                 qh_ref, kh_ref, vh_ref, k_ref, v_ref, lf_ref, xn_s, *, transpose_qv):
    j = pl.program_id(1)

    @pl.when(j == 0)
    def _():
        xn = _rms(h_ref[...], g_ref[...]).astype(BF16)
        xn_s[...] = xn
        z = _dot(xn, wf_ref[...]) + bf_ref[...]
        lf_ref[...] = jnp.minimum(z, 0.0) - jnp.log1p(jnp.exp(-jnp.abs(z)))

    res = _dot(xn_s[...], w_ref[...])

    def heads(dst, scale=None, transpose=False):
        for hh in range(HEADS_PER_TILE):
            x = res[:, hh * HEAD_DIM:(hh + 1) * HEAD_DIM]
            if scale is not None:
                x = x * scale
            dst[hh] = (x.T if transpose else x).astype(BF16)

    @pl.when(j < TILES_PER_PROJ)
    def _():
        heads(qh_ref, Q_SCALE, transpose_qv)

    @pl.when((j >= TILES_PER_PROJ) & (j < 2 * TILES_PER_PROJ))
    def _():
        heads(kh_ref)
        k_ref[...] = res

    @pl.when(j >= 2 * TILES_PER_PROJ)
    def _():
        heads(vh_ref, None, transpose_qv)
        v_ref[...] = res


def _qkvf(h, g, w_qkv, w_f, b_f, *, layer, fox_idx, tm, transpose_qv):
    m = h.shape[0]
    tpp = TILES_PER_PROJ

    def sel(which):
        return lambda i, j: jnp.clip(j - which * tpp, 0, tpp - 1)

    hm_shape = jax.ShapeDtypeStruct((N_HEADS, m, HEAD_DIM), BF16)
    hm_t_shape = jax.ShapeDtypeStruct((N_HEADS, HEAD_DIM, m), BF16)

    def hm_t_spec(which):
        s = sel(which)
        return pl.BlockSpec((HEADS_PER_TILE, HEAD_DIM, tm), lambda i, j: (s(i, j), 0, i))
    flat_shape = jax.ShapeDtypeStruct((m, D_MODEL), F32)

    def hm_spec(which):
        s = sel(which)
        return pl.BlockSpec((HEADS_PER_TILE, tm, HEAD_DIM), lambda i, j: (s(i, j), i, 0))

    def flat_spec(which):
        s = sel(which)
        return pl.BlockSpec((tm, QKV_TILE), lambda i, j: (i, s(i, j)))

    qv_spec = hm_t_spec if transpose_qv else hm_spec
    qv_shape = hm_t_shape if transpose_qv else hm_shape
    return pl.pallas_call(
        functools.partial(_qkvf_kernel, transpose_qv=transpose_qv),
        grid=(m // tm, 3 * tpp),
        in_specs=[
            pl.BlockSpec((tm, D_MODEL), lambda i, j: (i, 0)),
            pl.BlockSpec((None, 1, D_MODEL), lambda i, j: (layer, 0, 0)),
            pl.BlockSpec((None, D_MODEL, QKV_TILE), lambda i, j: (fox_idx, 0, j)),
            pl.BlockSpec((None, D_MODEL, N_HEADS), lambda i, j: (fox_idx, 0, 0)),
            pl.BlockSpec((None, 1, N_HEADS), lambda i, j: (fox_idx, 0, 0)),
        ],
        out_specs=[qv_spec(0), hm_spec(1), qv_spec(2), flat_spec(1), flat_spec(2),
                   pl.BlockSpec((tm, N_HEADS), lambda i, j: (i, 0))],
        out_shape=[qv_shape, hm_shape, qv_shape, flat_shape, flat_shape,
                   jax.ShapeDtypeStruct((m, N_HEADS), F32)],
        scratch_shapes=[pltpu.VMEM((tm, D_MODEL), BF16)],
        compiler_params=_params(2),
        name="fox_qkvf",
    )(h, g, w_qkv, w_f, b_f)


def _lane_cumsum(x):
    n = x.shape[-1]
    lane = lax.broadcasted_iota(jnp.int32, x.shape, x.ndim - 1)
    k = 1
    while k < n:
        x = x + jnp.where(lane >= k, pltpu.roll(x, k, x.ndim - 1), 0.0)
        k *= 2
    return x


def _cumsum_kernel(x_ref, o_ref):
    o_ref[...] = _lane_cumsum(x_ref[...]) * LOG2E


def _cumsum_last(x):
    b, hh, s = x.shape
    return pl.pallas_call(
        _cumsum_kernel,
        grid=(b,),
        in_specs=[pl.BlockSpec((None, hh, s), lambda i: (i, 0, 0))],
        out_specs=pl.BlockSpec((None, hh, s), lambda i: (i, 0, 0)),
        out_shape=jax.ShapeDtypeStruct(x.shape, F32),
        compiler_params=_params(1),
        name="forget_cumsum",
    )(x)


AUG_DIM = 2 * HEAD_DIM
N_SPLIT = 3


def _split_bf16(x):
    parts = []
    for _ in range(N_SPLIT):
        piece = x.astype(BF16).astype(F32)
        parts.append(piece)
        x = x - piece
    return parts


def _augment_kernel(qt_ref, kh_ref, c_ref, ct_ref, qa_ref, ka_ref, *, tm):
    sub = lax.broadcasted_iota(jnp.int32, (HEAD_DIM, tm), 0)
    lane = lax.broadcasted_iota(jnp.int32, (tm, HEAD_DIM), 1)
    c = c_ref[...]
    ct = ct_ref[...]
    for h in range(N_HEADS):
        q_parts = _split_bf16(ct[h:h + 1, :])
        aug_q = jnp.where(sub < 2 * N_SPLIT, 1.0, 0.0)
        for n in range(N_SPLIT):
            aug_q = jnp.where(sub == n, q_parts[n], aug_q)
        qa_ref[h, 0:HEAD_DIM, :] = qt_ref[h]
        qa_ref[h, HEAD_DIM:AUG_DIM, :] = aug_q.astype(BF16)
        k_parts = _split_bf16(c[:, h:h + 1])
        aug_k = jnp.where(lane < N_SPLIT, 1.0, 0.0)
        for n in range(N_SPLIT):
            aug_k = jnp.where(lane == N_SPLIT + n, -k_parts[n], aug_k)
        ka_ref[h, :, 0:HEAD_DIM] = kh_ref[h]
        ka_ref[h, :, HEAD_DIM:AUG_DIM] = aug_k.astype(BF16)


def _augment(qt, kh, c, ct, *, tm):
    m = kh.shape[1]
    per_seq = ct.shape[2] // tm
    return pl.pallas_call(
        functools.partial(_augment_kernel, tm=tm),
        grid=(m // tm,),
        in_specs=[
            pl.BlockSpec((N_HEADS, HEAD_DIM, tm), lambda i: (0, 0, i)),
            pl.BlockSpec((N_HEADS, tm, HEAD_DIM), lambda i: (0, i, 0)),
            pl.BlockSpec((tm, N_HEADS), lambda i: (i, 0)),
            pl.BlockSpec((None, N_HEADS, tm), lambda i: (i // per_seq, 0, i % per_seq)),
        ],
        out_specs=[
            pl.BlockSpec((N_HEADS, AUG_DIM, tm), lambda i: (0, 0, i)),
            pl.BlockSpec((N_HEADS, tm, AUG_DIM), lambda i: (0, i, 0)),
        ],
        out_shape=[jax.ShapeDtypeStruct((N_HEADS, AUG_DIM, m), BF16),
                   jax.ShapeDtypeStruct((N_HEADS, m, AUG_DIM), BF16)],
        compiler_params=_params(1),
        name="fox_bias_augment",
    )(qt, kh, c, ct)


def _flash_kernel(q_ref, k_ref, v_ref, o_ref, m_s, l_s, acc_s, *, tq):
    qi = pl.program_id(1)
    ki = pl.program_id(2)

    @pl.when(ki == 0)
    def _():
        m_s[...] = jnp.full(m_s.shape, NEG_INF, F32)
        l_s[...] = jnp.zeros(l_s.shape, F32)
        acc_s[...] = jnp.zeros(acc_s.shape, F32)

    def step(diagonal):
        def body(h, carry):
            s = _dot(k_ref[h], q_ref[h])
            if diagonal:
                key = lax.broadcasted_iota(jnp.int32, (tq, tq), 0)
                qry = lax.broadcasted_iota(jnp.int32, (tq, tq), 1)
                s = jnp.where(key <= qry, s, NEG_INF)
            m_prev = m_s[h]
            m_new = jnp.maximum(m_prev, jnp.max(s, axis=0, keepdims=True))
            alpha = jnp.exp2(m_prev - m_new)
            p = jnp.exp2(s - m_new)
            l_s[h] = alpha * l_s[h] + jnp.sum(p, axis=0, keepdims=True)
            acc_s[h] = alpha * acc_s[h] + _dot(v_ref[h], p.astype(BF16))
            m_s[h] = m_new
            return carry

        lax.fori_loop(0, N_HEADS, body, 0, unroll=4)

    @pl.when(ki < qi)
    def _():
        step(False)

    @pl.when(ki == qi)
    def _():
        step(True)

        def fin(h, carry):
            o_ref[h] = (acc_s[h] / l_s[h]).T.astype(BF16)
            return carry

        lax.fori_loop(0, N_HEADS, fin, 0)


def _flash_prompt(q_aug, k_aug, vt, *, batch, seq, tq):
    nq = seq // tq
    m = k_aug.shape[1]

    def kv_tile(b, qi, ki):
        return b * nq + jnp.minimum(ki, qi)

    row = pltpu.VMEM((N_HEADS, 1, tq), F32)
    return pl.pallas_call(
        functools.partial(_flash_kernel, tq=tq),
        grid=(batch, nq, nq),
        in_specs=[
            pl.BlockSpec((N_HEADS, AUG_DIM, tq), lambda b, qi, ki: (0, 0, b * nq + qi)),
            pl.BlockSpec((N_HEADS, tq, AUG_DIM), lambda b, qi, ki: (0, kv_tile(b, qi, ki), 0)),
            pl.BlockSpec((N_HEADS, HEAD_DIM, tq), lambda b, qi, ki: (0, 0, kv_tile(b, qi, ki))),
        ],
        out_specs=pl.BlockSpec((N_HEADS, tq, HEAD_DIM), lambda b, qi, ki: (0, b * nq + qi, 0)),
        out_shape=jax.ShapeDtypeStruct((N_HEADS, m, HEAD_DIM), BF16),
        scratch_shapes=[row, row, pltpu.VMEM((N_HEADS, HEAD_DIM, tq), F32)],
        compiler_params=_params(3),
        name="fox_prompt_attention",
    )(q_aug, k_aug, vt)


DEC_T = 4
DEC_ROWS = DEC_T * N_HEADS
PAGE_ROWS = PAGE_SIZE * N_HEADS
LF_ROWS = PAGE_ROWS // LANES
PAGES_PER_STEP = 8


def _row_to_col(r):
    n = r.shape[-1]
    eye = (lax.broadcasted_iota(jnp.int32, (n, n), 0) == lax.broadcasted_iota(jnp.int32, (n, n), 1))
    return jnp.sum(jnp.where(eye, jnp.broadcast_to(r, (n, n)), 0.0), axis=1, keepdims=True)


def _strided_prefix(x, lane, steps):
    for k in steps:
        x = x + jnp.where(lane >= k, pltpu.roll(x, k, 1), 0.0)
    return x


def _decode_kernel(pt_ref, q_ref, kn_ref, vn_ref, lfn_ref, *rest, pps):
    k_refs, v_refs, lf_refs = rest[:pps], rest[pps:2 * pps], rest[2 * pps:3 * pps]
    o_ref, m_s, l_s, acc_s, carry_s, mask_s = rest[3 * pps:]
    pg = pl.program_id(1)
    lane = lax.broadcasted_iota(jnp.int32, (1, LANES), 1)
    q = q_ref[...]

    @pl.when(pg == 0)
    def _():
        cn_row = _strided_prefix(lfn_ref[...] * LOG2E, lane, (N_HEADS, 2 * N_HEADS))
        cn_col = _row_to_col(cn_row)[:DEC_ROWS]
        row_h = lax.broadcasted_iota(jnp.int32, (DEC_ROWS, PAGE_ROWS), 0) % N_HEADS
        col_h = lax.broadcasted_iota(jnp.int32, (DEC_ROWS, PAGE_ROWS), 1) % N_HEADS
        mask_s[...] = jnp.where(row_h == col_h, 0.0, NEG_INF) + cn_col
        s = _dot_nt(q, kn_ref[...])
        r = lax.broadcasted_iota(jnp.int32, (DEC_ROWS, LANES), 0)
        c = lax.broadcasted_iota(jnp.int32, (DEC_ROWS, LANES), 1)
        valid = (c < DEC_ROWS) & (c % N_HEADS == r % N_HEADS) & (c // N_HEADS <= r // N_HEADS)
        s = jnp.where(valid, s + (cn_col - cn_row), NEG_INF)
        m0 = jnp.max(s, axis=-1, keepdims=True)
        p0 = jnp.exp2(s - m0)
        m_s[...] = m0
        l_s[...] = jnp.sum(p0, axis=-1, keepdims=True)
        acc_s[...] = _dot(p0.astype(BF16), vn_ref[...])
        carry_s[...] = jnp.zeros(carry_s.shape, F32)

    sub = lax.broadcasted_iota(jnp.int32, (LF_ROWS, LANES), 0)
    lane_steps = (N_HEADS, 2 * N_HEADS, 4 * N_HEADS)
    carry = carry_s[...]
    logits = []
    for u in range(pps):
        x = _strided_prefix(lf_refs[u][...] * LOG2E, lane, lane_steps)
        y = jnp.where(lane >= LANES - N_HEADS, x, 0.0)
        for k in lane_steps:
            y = y + pltpu.roll(y, k, 1)
        e = y
        k = 1
        while k < LF_ROWS:
            e = e + jnp.where(sub >= k, pltpu.roll(e, k, 0), 0.0)
            k *= 2
        tot = e[LF_ROWS - 1:LF_ROWS, :]
        suffix = (tot - (x + (e - y))) + carry
        carry = carry + tot
        bias = jnp.concatenate([suffix[r:r + 1, :] for r in range(LF_ROWS)], axis=1)
        kp = k_refs[u][...].reshape(PAGE_ROWS, HEAD_DIM).astype(BF16)
        logits.append(_dot_nt(q, kp) + bias + mask_s[...])
    carry_s[...] = carry

    m_prev = m_s[...]
    m_new = m_prev
    for s in logits:
        m_new = jnp.maximum(m_new, jnp.max(s, axis=-1, keepdims=True))
    alpha = jnp.exp2(m_prev - m_new)
    l_new = alpha * l_s[...]
    acc = alpha * acc_s[...]
    for u in range(pps):
        p = jnp.exp2(logits[u] - m_new)
        l_new = l_new + jnp.sum(p, axis=-1, keepdims=True)
        vp = v_refs[u][...].reshape(PAGE_ROWS, HEAD_DIM).astype(BF16)
        acc = acc + _dot(p.astype(BF16), vp)
    m_s[...] = m_new
    l_s[...] = l_new
    acc_s[...] = acc

    @pl.when(pg == pl.num_programs(1) - 1)
    def _():
        o_ref[...] = acc / l_new


def _decode_attention(page_table, q, k_new, v_new, lf_new, cache_k, cache_v, cache_lf, layer):
    nb, n_pages = page_table.shape
    pps = PAGES_PER_STEP
    assert n_pages % pps == 0

    def new_map(b, pg, pt):
        return (b, 0, 0)

    def page_spec(block, u):
        zeros = (0,) * (len(block) - 2)
        return pl.BlockSpec(
            block, lambda b, pg, pt: (layer, pt[b, n_pages - 1 - (pg * pps + u)]) + zeros)

    kv_block = (None, None, PAGE_SIZE, N_HEADS, HEAD_DIM)
    lf_block = (None, None, LF_ROWS, LANES)
    col = pltpu.VMEM((DEC_ROWS, 1), F32)
    grid_spec = pltpu.PrefetchScalarGridSpec(
        num_scalar_prefetch=1,
        grid=(nb, n_pages // pps),
        in_specs=[
            pl.BlockSpec((None, DEC_ROWS, HEAD_DIM), new_map),
            pl.BlockSpec((None, LANES, HEAD_DIM), new_map),
            pl.BlockSpec((None, LANES, HEAD_DIM), new_map),
            pl.BlockSpec((None, 1, LANES), new_map),
            *[page_spec(kv_block, u) for u in range(pps)],
            *[page_spec(kv_block, u) for u in range(pps)],
            *[page_spec(lf_block, u) for u in range(pps)],
        ],
        out_specs=pl.BlockSpec((None, DEC_ROWS, HEAD_DIM), new_map),
        scratch_shapes=[col, col, pltpu.VMEM((DEC_ROWS, HEAD_DIM), F32),
                        pltpu.VMEM((1, LANES), F32),
                        pltpu.VMEM((DEC_ROWS, PAGE_ROWS), F32)],
    )
    return pl.pallas_call(
        functools.partial(_decode_kernel, pps=pps),
        grid_spec=grid_spec,
        out_shape=jax.ShapeDtypeStruct((nb, DEC_ROWS, HEAD_DIM), F32),
        compiler_params=_params(2),
        name="fox_sample_attention",
    )(page_table, q, k_new, v_new, lf_new,
      *([cache_k] * pps), *([cache_v] * pps), *([cache_lf] * pps))


OUT_TILE = 512


def _oproj_kernel(o_ref, w_ref, h_ref, out_ref, *, head_major):
    if head_major:
        o = jnp.concatenate([o_ref[h] for h in range(N_HEADS)], axis=-1)
    else:
        o = o_ref[...].astype(BF16)
    out_ref[...] = h_ref[...] + _dot(o, w_ref[...])


def _oproj(o, w, h, *, fox_idx, tm, head_major):
    m = h.shape[0]
    if head_major:
        o_spec = pl.BlockSpec((N_HEADS, tm, HEAD_DIM), lambda i, j: (0, i, 0))
    else:
        o_spec = pl.BlockSpec((tm, D_MODEL), lambda i, j: (i, 0))
    return pl.pallas_call(
        functools.partial(_oproj_kernel, head_major=head_major),
        grid=(m // tm, D_MODEL // OUT_TILE),
        in_specs=[o_spec,
                  pl.BlockSpec((None, D_MODEL, OUT_TILE), lambda i, j: (fox_idx, 0, j)),
                  pl.BlockSpec((tm, OUT_TILE), lambda i, j: (i, j))],
        out_specs=pl.BlockSpec((tm, OUT_TILE), lambda i, j: (i, j)),
        out_shape=jax.ShapeDtypeStruct((m, D_MODEL), F32),
        compiler_params=_params(2),
        name="fox_out_proj",
    )(o, w, h)


N_OUT_TILES = D_MODEL // OUT_TILE


def _ffn_ple_kernel(h_ref, g_ref, wu_ref, wv_ref, cw_ref, cb_ref, wd_ref, prev_ref,
                    gp_ref, p_ref, wpe_ref, wpg_ref, gf_ref,
                    out_ref, tail_ref, xn_s, p_s, ue_s, carry_s,
                    *, tm, shift, keep, tiles_per_seq, nj, final_norm):
    i = pl.program_id(0)
    j = pl.program_id(1)

    @pl.when(j == 0)
    def _():
        xn_s[...] = _rms(h_ref[...], g_ref[...]).astype(BF16)
        out_ref[...] = h_ref[...]

    @pl.when(j < nj)
    def _():
        @pl.when(i % tiles_per_seq == 0)
        def _():
            carry_s[j] = prev_ref[...]

        xn = xn_s[...]
        u = _dot(xn, wu_ref[...])
        v = _dot(xn, wv_ref[...])
        ue_s[0:keep, :] = carry_s[j]
        ue_s[keep:keep + tm, :] = u
        cw = cw_ref[...]
        c = cb_ref[...]
        c = c + cw[0:1, :] * ue_s[keep - 2 * shift:keep - 2 * shift + tm, :]
        c = c + cw[1:2, :] * ue_s[keep - shift:keep - shift + tm, :]
        c = c + cw[2:3, :] * u
        gelu = 0.5 * c * (1.0 + jnp.tanh(0.7978845608028654 * (c + 0.044715 * (c * c * c))))
        out_ref[...] += _dot((gelu * v).astype(BF16), wd_ref[...])
        tail = ue_s[tm:tm + keep, :]
        carry_s[j] = tail
        tail_ref[...] = tail

    @pl.when(j == nj - 1)
    def _():
        xn_s[...] = _rms(out_ref[...], gp_ref[...]).astype(BF16)
        p_s[...] = p_ref[...].astype(BF16)

    for n in range(N_OUT_TILES):
        @pl.when(j == nj + n)
        def _(n=n):
            cols = slice(n * OUT_TILE, (n + 1) * OUT_TILE)
            e = _dot(p_s[...], wpe_ref[...])
            z = _dot(xn_s[...], wpg_ref[...])
            out_ref[:, cols] = out_ref[:, cols] + e * jax.nn.sigmoid(z)
            if final_norm and n == N_OUT_TILES - 1:
                out_ref[...] = _rms(out_ref[...], gf_ref[...])


def _ffn_ple_layer(h, g, w_uv, conv_w, conv_b, w_d, prev0, g_ple, p, w_pe, w_pg, g_final,
                   *, layer, tm, shift, seq_rows, final_norm):
    m = h.shape[0]
    ff = w_d.shape[1]
    pd = p.shape[2]
    nj = ff // FF_TILE
    keep = prev0.shape[2]
    tps = seq_rows // tm
    assert keep >= CONV_CTX * shift and tm >= keep and keep % SUBLANES == 0
    kern = functools.partial(_ffn_ple_kernel, tm=tm, shift=shift, keep=keep, tiles_per_seq=tps,
                             nj=nj, final_norm=final_norm)

    def ffj(j):
        return jnp.minimum(j, nj - 1)

    def plej(j):
        return jnp.clip(j - nj, 0, N_OUT_TILES - 1)

    row = pl.BlockSpec((None, 1, D_MODEL), lambda i, j: (layer, 0, 0))
    return pl.pallas_call(
        kern,
        grid=(m // tm, nj + N_OUT_TILES),
        in_specs=[
            pl.BlockSpec((tm, D_MODEL), lambda i, j: (i, 0)),
            row,
            pl.BlockSpec((None, D_MODEL, FF_TILE), lambda i, j: (layer, 0, ffj(j))),
            pl.BlockSpec((None, D_MODEL, FF_TILE), lambda i, j: (layer, 0, nj + ffj(j))),
            pl.BlockSpec((None, CONV_WIDTH, FF_TILE), lambda i, j: (layer, 0, ffj(j))),
            pl.BlockSpec((None, 1, FF_TILE), lambda i, j: (layer, 0, ffj(j))),
            pl.BlockSpec((None, FF_TILE, D_MODEL), lambda i, j: (layer, ffj(j), 0)),
            pl.BlockSpec((None, None, keep, FF_TILE), lambda i, j: (layer, i // tps, 0, ffj(j))),
            row,
            pl.BlockSpec((None, tm, pd), lambda i, j: (layer, i, 0)),
            pl.BlockSpec((None, pd, OUT_TILE), lambda i, j: (layer, 0, plej(j))),
            pl.BlockSpec((None, D_MODEL, OUT_TILE), lambda i, j: (layer, 0, plej(j))),
            pl.BlockSpec((1, D_MODEL), lambda i, j: (0, 0)),
        ],
        out_specs=[
            pl.BlockSpec((tm, D_MODEL), lambda i, j: (i, 0)),
            pl.BlockSpec((None, keep, FF_TILE), lambda i, j: (i, 0, ffj(j))),
        ],
        out_shape=[jax.ShapeDtypeStruct((m, D_MODEL), F32),
                   jax.ShapeDtypeStruct((m // tm, keep, ff), F32)],
        scratch_shapes=[pltpu.VMEM((tm, D_MODEL), BF16),
                        pltpu.VMEM((tm, pd), BF16),
                        pltpu.VMEM((keep + tm, FF_TILE), F32),
                        pltpu.VMEM((nj, keep, FF_TILE), F32)],
        compiler_params=_params(2),
        name="conv_ffn_ple",
    )(h, g, w_uv, w_uv, conv_w, conv_b, w_d, prev0, g_ple, p, w_pe, w_pg, g_final)


CAST_ROWS = 256


def _cast_up_kernel(w_ref, o_ref, *, d_ff, ff_pad):
    o_ref[:, 0:d_ff] = w_ref[:, 0:d_ff].astype(BF16)
    o_ref[:, d_ff:ff_pad] = jnp.zeros((o_ref.shape[0], ff_pad - d_ff), BF16)
    o_ref[:, ff_pad:ff_pad + d_ff] = w_ref[:, d_ff:2 * d_ff].astype(BF16)
    o_ref[:, ff_pad + d_ff:2 * ff_pad] = jnp.zeros((o_ref.shape[0], ff_pad - d_ff), BF16)


def _cast_up(w_up, ff_pad):
    depth, d, two_ff = w_up.shape
    d_ff = two_ff // 2
    return pl.pallas_call(
        functools.partial(_cast_up_kernel, d_ff=d_ff, ff_pad=ff_pad),
        grid=(depth, d // CAST_ROWS),
        in_specs=[pl.BlockSpec((None, CAST_ROWS, two_ff), lambda l, r: (l, r, 0))],
        out_specs=pl.BlockSpec((None, CAST_ROWS, 2 * ff_pad), lambda l, r: (l, r, 0)),
        out_shape=jax.ShapeDtypeStruct((depth, d, 2 * ff_pad), BF16),
        compiler_params=_params(2),
        name="cast_w_up",
    )(w_up)


def _cast_down_kernel(w_ref, o_ref, *, d_ff):
    o_ref[0:d_ff, :] = w_ref[...].astype(BF16)
    o_ref[d_ff:, :] = jnp.zeros((o_ref.shape[0] - d_ff, o_ref.shape[1]), BF16)


def _cast_down(w_down, ff_pad):
    depth, d_ff, d = w_down.shape
    return pl.pallas_call(
        functools.partial(_cast_down_kernel, d_ff=d_ff),
        grid=(depth, d // OUT_TILE),
        in_specs=[pl.BlockSpec((None, d_ff, OUT_TILE), lambda l, c: (l, 0, c))],
        out_specs=pl.BlockSpec((None, ff_pad, OUT_TILE), lambda l, c: (l, 0, c)),
        out_shape=jax.ShapeDtypeStruct((depth, ff_pad, d), BF16),
        compiler_params=_params(2),
        name="cast_w_down",
    )(w_down)


PROMPT_TILE = 512
QKV_ROWS = 1024
FLASH_TILE = 512


def _trunk(h, p, wts, *, batch, steps, time_major, pos0, pool_prev, conv_prev, decode):
    depth = p.shape[0]
    m = h.shape[0]
    if time_major:
        tm, tm_qkv, shift, seq_rows = m, m, batch, m
    else:
        tm, tm_qkv, shift, seq_rows = PROMPT_TILE, QKV_ROWS, 1, steps
    pool_tails, conv_tails, ks, vs, lfs = [], [], [], [], []
    for i in range(depth):
        j = i // 2
        if i % 2 == 0:
            h, tail = _pool_layer(h, wts["g_mix"], pool_prev, wts["w_pool"], wts["pool_scale"],
                                  layer=i, pool_idx=j, tm=tm, shift=shift,
                                  seq_rows=seq_rows, pos0=pos0)
            pool_tails.append(tail)
        else:
            qh, kh, vh, k, v, lf = _qkvf(
                h, wts["g_mix"], wts["w_qkvf"], wts["w_f"], wts["b_f"],
                layer=i, fox_idx=j, tm=tm_qkv, transpose_qv=decode is None)
            ks.append(k)
            vs.append(v)
            lfs.append(lf)
            if decode is None:
                lf_t = lf.reshape(batch, steps, N_HEADS).transpose(0, 2, 1)
                ct = _cumsum_last(lf_t)
                c = ct.transpose(0, 2, 1).reshape(m, N_HEADS)
                q_aug, k_aug = _augment(qh, kh, c, ct, tm=FLASH_TILE)
                o = _flash_prompt(q_aug, k_aug, vh, batch=batch, seq=steps, tq=FLASH_TILE)
                h = _oproj(o, wts["w_o"], h, fox_idx=j, tm=tm, head_major=True)
            else:
                page_table, cache_k, cache_v, cache_lf = decode

                def per_seq(a):
                    a = a.reshape(steps, batch, N_HEADS, -1).transpose(1, 0, 2, 3)
                    return a.reshape(batch, steps * N_HEADS, -1)

                def as_page(a):
                    return jnp.pad(a, ((0, 0), (0, LANES - DEC_ROWS), (0, 0))).astype(BF16)

                q_rows = per_seq(qh.transpose(1, 0, 2).reshape(m, D_MODEL))
                lf_rows = per_seq(lf).reshape(batch, 1, DEC_ROWS)
                lf_rows = jnp.pad(lf_rows, ((0, 0), (0, 0), (0, LANES - DEC_ROWS)))
                o = _decode_attention(page_table, q_rows, as_page(per_seq(k)), as_page(per_seq(v)),
                                      lf_rows, cache_k, cache_v, cache_lf, j)
                o = o.reshape(batch, steps, D_MODEL).transpose(1, 0, 2).reshape(m, D_MODEL)
                h = _oproj(o, wts["w_o"], h, fox_idx=j, tm=tm, head_major=False)
        h, ctail = _ffn_ple_layer(
            h, wts["g_ffn"], wts["w_uv"], wts["conv_w"], wts["conv_b"], wts["w_d"], conv_prev,
            wts["g_ple"], p, wts["w_pe"], wts["w_pg"], wts["g_final"],
            layer=i, tm=tm, shift=shift, seq_rows=seq_rows, final_norm=(i == depth - 1))
        conv_tails.append(ctail)
    return h, pool_tails, conv_tails, ks, vs, lfs


def kernel(x_prompt, x_sample, state_pool, state_conv, cache_k, cache_v, cache_lf, page_table,
           p_prompt, p_sample, g_mix, w_pool, pool_scale, w_qkvf, b_f, w_o,
           g_ffn, w_up, conv_w, conv_b, w_down, w_pe, g_ple, w_pg, g_final):
    depth = g_mix.shape[0]
    bp, sp, _ = x_prompt.shape
    bs, ts, _ = x_sample.shape
    d_ff = w_down.shape[1]
    ff_pad = pl.cdiv(d_ff, FF_TILE) * FF_TILE
    padc = ff_pad - d_ff
    n_pool = w_pool.shape[0]
    n_fox = w_o.shape[0]
    assert ts == DEC_T

    w_qkvf_b = w_qkvf.astype(BF16)
    wts = {
        "g_mix": g_mix[:, None, :], "g_ffn": g_ffn[:, None, :], "g_ple": g_ple[:, None, :],
        "g_final": g_final[None, :],
        "w_pool": w_pool.astype(BF16), "pool_scale": pool_scale[:, None, :],
        "w_qkvf": w_qkvf_b,
        "w_f": w_qkvf_b[:, :, 3 * D_MODEL:],
        "b_f": b_f[:, None, :],
        "w_o": w_o.astype(BF16),
        "w_uv": _cast_up(w_up, ff_pad),
        "conv_w": jnp.pad(conv_w, ((0, 0), (0, 0), (0, padc))),
        "conv_b": jnp.pad(conv_b, ((0, 0), (0, padc)))[:, None, :],
        "w_d": _cast_down(w_down, ff_pad),
        "w_pe": w_pe.astype(BF16), "w_pg": w_pg.astype(BF16),
    }

    pool_prev_p = jnp.zeros((n_pool, bp, POOL_SPAN, D_MODEL), F32)
    conv_prev_p = jnp.zeros((depth, bp, SUBLANES, ff_pad), F32)
    y_p, pool_p, conv_p, k_p, v_p, lf_p = _trunk(
        x_prompt.reshape(bp * sp, D_MODEL), p_prompt.reshape(depth, bp * sp, -1), wts,
        batch=bp, steps=sp, time_major=False, pos0=0,
        pool_prev=pool_prev_p, conv_prev=conv_prev_p, decode=None)

    n_pages = page_table.shape[1]
    pool_prev_s = jnp.pad(state_pool.transpose(0, 2, 1, 3), ((0, 0), (1, 0), (0, 0), (0, 0)))
    pool_prev_s = pool_prev_s.reshape(n_pool, 1, POOL_SPAN * bs, D_MODEL)
    conv_prev_s = jnp.pad(state_conv.transpose(0, 2, 1, 3), ((0, 0), (0, 0), (0, 0), (0, padc)))
    conv_prev_s = conv_prev_s.reshape(depth, 1, CONV_CTX * bs, ff_pad)
    n_phys = cache_k.shape[1]
    decode = (page_table, cache_k, cache_v,
              cache_lf.reshape(n_fox, n_phys, LF_ROWS, LANES))
    y_s, pool_s, conv_s, k_s, v_s, lf_s = _trunk(
        x_sample.transpose(1, 0, 2).reshape(ts * bs, D_MODEL),
        p_sample.transpose(0, 2, 1, 3).reshape(depth, ts * bs, -1), wts,
        batch=bs, steps=ts, time_major=True, pos0=n_pages * PAGE_SIZE,
        pool_prev=pool_prev_s, conv_prev=conv_prev_s, decode=decode)

    def from_tm(a, lead):
        return a.reshape(lead, bs, -1).transpose(1, 0, 2)

    ctx = POOL_SPAN - 1
    out_pool_p = jnp.stack([t[:, 1:, :] for t in pool_p])
    out_pool_s = jnp.stack([from_tm(t[0], POOL_SPAN)[:, 1:, :] for t in pool_s])
    tps = sp // PROMPT_TILE
    out_conv_p = jnp.stack([t[tps - 1::tps, SUBLANES - CONV_CTX:, :d_ff] for t in conv_p])
    out_conv_s = jnp.stack([from_tm(t[0], CONV_CTX)[:, :, :d_ff] for t in conv_s])
    assert out_pool_p.shape[2] == ctx

    def heads_p(a):
        return a.reshape(bp, sp, N_HEADS, HEAD_DIM)

    def heads_s(a):
        return from_tm(a, ts).reshape(bs, ts, N_HEADS, HEAD_DIM)

    return (y_p.reshape(bp, sp, D_MODEL),
            from_tm(y_s, ts),
            out_pool_p, out_pool_s, out_conv_p, out_conv_s,
            jnp.stack([heads_p(a) for a in k_p]),
            jnp.stack([heads_p(a) for a in v_p]),
            jnp.stack([a.reshape(bp, sp, N_HEADS) for a in lf_p]),
            jnp.stack([heads_s(a) for a in k_s]),
            jnp.stack([heads_s(a) for a in v_s]),
            jnp.stack([from_tm(a, ts) for a in lf_s]))
```

```python
import functools

import jax
import jax.numpy as jnp
from jax import lax
from jax.experimental import pallas as pl
from jax.experimental.pallas import tpu as pltpu

D_MODEL = 2048
N_HEADS = 16
HEAD_DIM = D_MODEL // N_HEADS
POOL_WINDOWS = (2, 4, 8, 16)
POOL_GROUP_DIM = D_MODEL // len(POOL_WINDOWS)
POOL_SPAN = max(POOL_WINDOWS)
CONV_WIDTH = 3
CONV_CTX = CONV_WIDTH - 1
PAGE_SIZE = 128
RMS_EPS = 1e-6
ATTN_SCALE = HEAD_DIM ** -0.5
LOG2E = 1.4426950408889634
Q_SCALE = ATTN_SCALE * LOG2E

LANES = 128
SUBLANES = 8
FF_TILE = 512
VMEM_LIMIT = 56 * 1024 * 1024

F32 = jnp.float32
BF16 = jnp.bfloat16
NEG_INF = float("-inf")


def _params(n_axes):
    return pltpu.CompilerParams(dimension_semantics=("arbitrary",) * n_axes,
                                vmem_limit_bytes=VMEM_LIMIT)


def _rms(x, g):
    return x * lax.rsqrt(jnp.mean(x * x, axis=-1, keepdims=True) + RMS_EPS) * g


def _dot(a, b):
    return jnp.dot(a, b, preferred_element_type=F32)


def _dot_nt(a, b):
    return lax.dot_general(a, b, (((1,), (1,)), ((), ())), preferred_element_type=F32)


def _pool_kernel(h_ref, g_ref, prev_ref, w_ref, sc_ref, o_ref, tail_ref, ext_s,
                 *, tm, shift, tiles_per_seq, pos0):
    i = pl.program_id(0)
    span = POOL_SPAN * shift
    it = i % tiles_per_seq

    @pl.when(it == 0)
    def _():
        ext_s[0:span, :] = prev_ref[...]

    x = h_ref[...]
    xn = _rms(x, g_ref[...])
    ext_s[span:span + tm, :] = xn
    row = lax.broadcasted_iota(jnp.int32, (tm, 1), 0)
    pos = pos0 + (it * tm + row) // shift
    for g, w in enumerate(POOL_WINDOWS):
        cols = slice(g * POOL_GROUP_DIM, (g + 1) * POOL_GROUP_DIM)
        acc = xn[:, cols]
        for k in range(1, w):
            acc = acc + ext_s[span - k * shift:span - k * shift + tm, cols]
        cnt = jnp.minimum(w, pos + 1).astype(F32)
        delta = acc / cnt - xn[:, cols]
        y = _dot(delta.astype(BF16), w_ref[g])
        o_ref[:, cols] = x[:, cols] + y * sc_ref[:, cols]
    tail = ext_s[tm:tm + span, :]
    tail_ref[...] = tail
    if tiles_per_seq > 1:
        ext_s[0:span, :] = tail


def _pool_layer(h, g, prev0, w_pool, scale, *, layer, pool_idx, tm, shift, seq_rows, pos0):
    m = h.shape[0]
    span = POOL_SPAN * shift
    tps = seq_rows // tm
    nseq = m // seq_rows
    assert tps == 1 or tm >= span
    kern = functools.partial(_pool_kernel, tm=tm, shift=shift, tiles_per_seq=tps, pos0=pos0)
    return pl.pallas_call(
        kern,
        grid=(m // tm,),
        in_specs=[
            pl.BlockSpec((tm, D_MODEL), lambda i: (i, 0)),
            pl.BlockSpec((None, 1, D_MODEL), lambda i: (layer, 0, 0)),
            pl.BlockSpec((None, None, span, D_MODEL), lambda i: (pool_idx, i // tps, 0, 0)),
            pl.BlockSpec((None, len(POOL_WINDOWS), POOL_GROUP_DIM, POOL_GROUP_DIM),
                         lambda i: (pool_idx, 0, 0, 0)),
            pl.BlockSpec((None, 1, D_MODEL), lambda i: (pool_idx, 0, 0)),
        ],
        out_specs=[
            pl.BlockSpec((tm, D_MODEL), lambda i: (i, 0)),
            pl.BlockSpec((None, span, D_MODEL), lambda i: (i // tps, 0, 0)),
        ],
        out_shape=[jax.ShapeDtypeStruct((m, D_MODEL), F32),
                   jax.ShapeDtypeStruct((nseq, span, D_MODEL), F32)],
        scratch_shapes=[pltpu.VMEM((span + tm, D_MODEL), F32)],
        compiler_params=_params(1),
        name="pool_layer",
    )(h, g, prev0, w_pool, scale)


QKV_TILE = 512
HEADS_PER_TILE = QKV_TILE // HEAD_DIM
TILES_PER_PROJ = D_MODEL // QKV_TILE


def _qkvf_kernel(h_ref, g_ref, w_ref, wf_ref, bf_ref, k_all_ref, v_all_ref,
                 qh_ref, kh_ref, vh_ref, k_ref, v_ref, lf_ref, xn_s, *, transpose_qv):
    del k_all_ref, v_all_ref
    j = pl.program_id(1)

    @pl.when(j == 0)
    def _():
        xn = _rms(h_ref[...], g_ref[...]).astype(BF16)
        xn_s[...] = xn
        z = _dot(xn, wf_ref[...]) + bf_ref[...]
        lf_ref[...] = jnp.minimum(z, 0.0) - jnp.log1p(jnp.exp(-jnp.abs(z)))

    res = _dot(xn_s[...], w_ref[...])

    def heads(dst, scale=None, transpose=False):
        for hh in range(HEADS_PER_TILE):
            x = res[:, hh * HEAD_DIM:(hh + 1) * HEAD_DIM]
            if scale is not None:
                x = x * scale
            dst[hh] = (x.T if transpose else x).astype(BF16)

    @pl.when(j < TILES_PER_PROJ)
    def _():
        heads(qh_ref, Q_SCALE, transpose_qv)

    @pl.when((j >= TILES_PER_PROJ) & (j < 2 * TILES_PER_PROJ))
    def _():
        heads(kh_ref)
        k_ref[...] = res

    @pl.when(j >= 2 * TILES_PER_PROJ)
    def _():
        heads(vh_ref, None, transpose_qv)
        v_ref[...] = res


def _qkvf(h, g, w_qkv, w_f, b_f, k_all, v_all, *, layer, fox_idx, tm, transpose_qv):
    m = h.shape[0]
    tpp = TILES_PER_PROJ

    def sel(which):
        return lambda i, j: jnp.clip(j - which * tpp, 0, tpp - 1)

    hm_shape = jax.ShapeDtypeStruct((N_HEADS, m, HEAD_DIM), BF16)
    hm_t_shape = jax.ShapeDtypeStruct((N_HEADS, HEAD_DIM, m), BF16)

    def hm_t_spec(which):
        s = sel(which)
        return pl.BlockSpec((HEADS_PER_TILE, HEAD_DIM, tm), lambda i, j: (s(i, j), 0, i))
    flat_shape = jax.ShapeDtypeStruct(k_all.shape, F32)

    def hm_spec(which):
        s = sel(which)
        return pl.BlockSpec((HEADS_PER_TILE, tm, HEAD_DIM), lambda i, j: (s(i, j), i, 0))

    def flat_spec(which):
        s = sel(which)
        return pl.BlockSpec((None, tm, QKV_TILE), lambda i, j: (fox_idx, i, s(i, j)))

    qv_spec = hm_t_spec if transpose_qv else hm_spec
    qv_shape = hm_t_shape if transpose_qv else hm_shape
    return pl.pallas_call(
        functools.partial(_qkvf_kernel, transpose_qv=transpose_qv),
        grid=(m // tm, 3 * tpp),
        in_specs=[
            pl.BlockSpec((tm, D_MODEL), lambda i, j: (i, 0)),
            pl.BlockSpec((None, 1, D_MODEL), lambda i, j: (layer, 0, 0)),
            pl.BlockSpec((None, D_MODEL, QKV_TILE), lambda i, j: (fox_idx, 0, j)),
            pl.BlockSpec((None, D_MODEL, N_HEADS), lambda i, j: (fox_idx, 0, 0)),
            pl.BlockSpec((None, 1, N_HEADS), lambda i, j: (fox_idx, 0, 0)),
            pl.BlockSpec(memory_space=pl.ANY),
            pl.BlockSpec(memory_space=pl.ANY),
        ],
        out_specs=[qv_spec(0), hm_spec(1), qv_spec(2), flat_spec(1), flat_spec(2),
                   pl.BlockSpec((tm, N_HEADS), lambda i, j: (i, 0))],
        out_shape=[qv_shape, hm_shape, qv_shape, flat_shape, flat_shape,
                   jax.ShapeDtypeStruct((m, N_HEADS), F32)],
        input_output_aliases={5: 3, 6: 4},
        scratch_shapes=[pltpu.VMEM((tm, D_MODEL), BF16)],
        compiler_params=_params(2),
        name="fox_qkvf",
    )(h, g, w_qkv, w_f, b_f, k_all, v_all)


def _lane_cumsum(x):
    n = x.shape[-1]
    lane = lax.broadcasted_iota(jnp.int32, x.shape, x.ndim - 1)
    k = 1
    while k < n:
        x = x + jnp.where(lane >= k, pltpu.roll(x, k, x.ndim - 1), 0.0)
        k *= 2
    return x


def _cumsum_kernel(x_ref, o_ref):
    o_ref[...] = _lane_cumsum(x_ref[...]) * LOG2E


def _cumsum_last(x):
    b, hh, s = x.shape
    return pl.pallas_call(
        _cumsum_kernel,
        grid=(b,),
        in_specs=[pl.BlockSpec((None, hh, s), lambda i: (i, 0, 0))],
        out_specs=pl.BlockSpec((None, hh, s), lambda i: (i, 0, 0)),
        out_shape=jax.ShapeDtypeStruct(x.shape, F32),
        compiler_params=_params(1),
        name="forget_cumsum",
    )(x)


AUG_DIM = 2 * HEAD_DIM
N_SPLIT = 3


def _split_bf16(x):
    parts = []
    for _ in range(N_SPLIT):
        piece = x.astype(BF16).astype(F32)
        parts.append(piece)
        x = x - piece
    return parts


def _augment_kernel(qt_ref, kh_ref, c_ref, ct_ref, qa_ref, ka_ref, *, tm):
    sub = lax.broadcasted_iota(jnp.int32, (HEAD_DIM, tm), 0)
    lane = lax.broadcasted_iota(jnp.int32, (tm, HEAD_DIM), 1)
    c = c_ref[...]
    ct = ct_ref[...]
    for h in range(N_HEADS):
        q_parts = _split_bf16(ct[h:h + 1, :])
        aug_q = jnp.where(sub < 2 * N_SPLIT, 1.0, 0.0)
        for n in range(N_SPLIT):
            aug_q = jnp.where(sub == n, q_parts[n], aug_q)
        qa_ref[h, 0:HEAD_DIM, :] = qt_ref[h]
        qa_ref[h, HEAD_DIM:AUG_DIM, :] = aug_q.astype(BF16)
        k_parts = _split_bf16(c[:, h:h + 1])
        aug_k = jnp.where(lane < N_SPLIT, 1.0, 0.0)
        for n in range(N_SPLIT):
            aug_k = jnp.where(lane == N_SPLIT + n, -k_parts[n], aug_k)
        ka_ref[h, :, 0:HEAD_DIM] = kh_ref[h]
        ka_ref[h, :, HEAD_DIM:AUG_DIM] = aug_k.astype(BF16)


def _augment(qt, kh, c, ct, *, tm):
    m = kh.shape[1]
    per_seq = ct.shape[2] // tm
    return pl.pallas_call(
        functools.partial(_augment_kernel, tm=tm),
        grid=(m // tm,),
        in_specs=[
            pl.BlockSpec((N_HEADS, HEAD_DIM, tm), lambda i: (0, 0, i)),
            pl.BlockSpec((N_HEADS, tm, HEAD_DIM), lambda i: (0, i, 0)),
            pl.BlockSpec((tm, N_HEADS), lambda i: (i, 0)),
            pl.BlockSpec((None, N_HEADS, tm), lambda i: (i // per_seq, 0, i % per_seq)),
        ],
        out_specs=[
            pl.BlockSpec((N_HEADS, AUG_DIM, tm), lambda i: (0, 0, i)),
            pl.BlockSpec((N_HEADS, tm, AUG_DIM), lambda i: (0, i, 0)),
        ],
        out_shape=[jax.ShapeDtypeStruct((N_HEADS, AUG_DIM, m), BF16),
                   jax.ShapeDtypeStruct((N_HEADS, m, AUG_DIM), BF16)],
        compiler_params=_params(1),
        name="fox_bias_augment",
    )(qt, kh, c, ct)


def _flash_kernel(q_ref, k_ref, v_ref, o_ref, m_s, l_s, acc_s, *, tq):
    qi = pl.program_id(1)
    ki = pl.program_id(2)

    @pl.when(ki == 0)
    def _():
        m_s[...] = jnp.full(m_s.shape, NEG_INF, F32)
        l_s[...] = jnp.zeros(l_s.shape, F32)
        acc_s[...] = jnp.zeros(acc_s.shape, F32)

    def step(diagonal):
        def body(h, carry):
            s = _dot(k_ref[h], q_ref[h])
            if diagonal:
                key = lax.broadcasted_iota(jnp.int32, (tq, tq), 0)
                qry = lax.broadcasted_iota(jnp.int32, (tq, tq), 1)
                s = jnp.where(key <= qry, s, NEG_INF)
            m_prev = m_s[h]
            m_new = jnp.maximum(m_prev, jnp.max(s, axis=0, keepdims=True))
            alpha = jnp.exp2(m_prev - m_new)
            p = jnp.exp2(s - m_new)
            l_s[h] = alpha * l_s[h] + jnp.sum(p, axis=0, keepdims=True)
            acc_s[h] = alpha * acc_s[h] + _dot(v_ref[h], p.astype(BF16))
            m_s[h] = m_new
            return carry

        lax.fori_loop(0, N_HEADS, body, 0, unroll=4)

    @pl.when(ki < qi)
    def _():
        step(False)

    @pl.when(ki == qi)
    def _():
        step(True)

        def fin(h, carry):
            o_ref[h] = (acc_s[h] / l_s[h]).T.astype(BF16)
            return carry

        lax.fori_loop(0, N_HEADS, fin, 0)


def _flash_prompt(q_aug, k_aug, vt, *, batch, seq, tq):
    nq = seq // tq
    m = k_aug.shape[1]

    def kv_tile(b, qi, ki):
        return b * nq + jnp.minimum(ki, qi)

    row = pltpu.VMEM((N_HEADS, 1, tq), F32)
    return pl.pallas_call(
        functools.partial(_flash_kernel, tq=tq),
        grid=(batch, nq, nq),
        in_specs=[
            pl.BlockSpec((N_HEADS, AUG_DIM, tq), lambda b, qi, ki: (0, 0, b * nq + qi)),
            pl.BlockSpec((N_HEADS, tq, AUG_DIM), lambda b, qi, ki: (0, kv_tile(b, qi, ki), 0)),
            pl.BlockSpec((N_HEADS, HEAD_DIM, tq), lambda b, qi, ki: (0, 0, kv_tile(b, qi, ki))),
        ],
        out_specs=pl.BlockSpec((N_HEADS, tq, HEAD_DIM), lambda b, qi, ki: (0, b * nq + qi, 0)),
        out_shape=jax.ShapeDtypeStruct((N_HEADS, m, HEAD_DIM), BF16),
        scratch_shapes=[row, row, pltpu.VMEM((N_HEADS, HEAD_DIM, tq), F32)],
        compiler_params=_params(3),
        name="fox_prompt_attention",
    )(q_aug, k_aug, vt)


DEC_T = 4
DEC_ROWS = DEC_T * N_HEADS
PAGE_ROWS = PAGE_SIZE * N_HEADS
LF_ROWS = PAGE_ROWS // LANES
PAGES_PER_STEP = 8


def _row_to_col(r):
    n = r.shape[-1]
    eye = (lax.broadcasted_iota(jnp.int32, (n, n), 0) == lax.broadcasted_iota(jnp.int32, (n, n), 1))
    return jnp.sum(jnp.where(eye, jnp.broadcast_to(r, (n, n)), 0.0), axis=1, keepdims=True)


def _strided_prefix(x, lane, steps):
    for k in steps:
        x = x + jnp.where(lane >= k, pltpu.roll(x, k, 1), 0.0)
    return x


def _decode_kernel(pt_ref, q_ref, kn_ref, vn_ref, lfn_ref, *rest, pps):
    k_refs, v_refs, lf_refs = rest[:pps], rest[pps:2 * pps], rest[2 * pps:3 * pps]
    o_ref, m_s, l_s, acc_s, carry_s, mask_s = rest[3 * pps:]
    pg = pl.program_id(1)
    lane = lax.broadcasted_iota(jnp.int32, (1, LANES), 1)
    q = q_ref[...]

    @pl.when(pg == 0)
    def _():
        cn_row = _strided_prefix(lfn_ref[...] * LOG2E, lane, (N_HEADS, 2 * N_HEADS))
        cn_col = _row_to_col(cn_row)[:DEC_ROWS]
        row_h = lax.broadcasted_iota(jnp.int32, (DEC_ROWS, PAGE_ROWS), 0) % N_HEADS
        col_h = lax.broadcasted_iota(jnp.int32, (DEC_ROWS, PAGE_ROWS), 1) % N_HEADS
        mask_s[...] = jnp.where(row_h == col_h, 0.0, NEG_INF) + cn_col
        s = _dot_nt(q, kn_ref[...])
        r = lax.broadcasted_iota(jnp.int32, (DEC_ROWS, LANES), 0)
        c = lax.broadcasted_iota(jnp.int32, (DEC_ROWS, LANES), 1)
        valid = (c < DEC_ROWS) & (c % N_HEADS == r % N_HEADS) & (c // N_HEADS <= r // N_HEADS)
        s = jnp.where(valid, s + (cn_col - cn_row), NEG_INF)
        m0 = jnp.max(s, axis=-1, keepdims=True)
        p0 = jnp.exp2(s - m0)
        m_s[...] = m0
        l_s[...] = jnp.sum(p0, axis=-1, keepdims=True)
        acc_s[...] = _dot(p0.astype(BF16), vn_ref[...])
        carry_s[...] = jnp.zeros(carry_s.shape, F32)

    sub = lax.broadcasted_iota(jnp.int32, (LF_ROWS, LANES), 0)
    lane_steps = (N_HEADS, 2 * N_HEADS, 4 * N_HEADS)
    carry = carry_s[...]
    logits = []
    for u in range(pps):
        x = _strided_prefix(lf_refs[u][...] * LOG2E, lane, lane_steps)
        y = jnp.where(lane >= LANES - N_HEADS, x, 0.0)
        for k in lane_steps:
            y = y + pltpu.roll(y, k, 1)
        e = y
        k = 1
        while k < LF_ROWS:
            e = e + jnp.where(sub >= k, pltpu.roll(e, k, 0), 0.0)
            k *= 2
        tot = e[LF_ROWS - 1:LF_ROWS, :]
        suffix = (tot - (x + (e - y))) + carry
        carry = carry + tot
        bias = jnp.concatenate([suffix[r:r + 1, :] for r in range(LF_ROWS)], axis=1)
        kp = k_refs[u][...].reshape(PAGE_ROWS, HEAD_DIM).astype(BF16)
        logits.append(_dot_nt(q, kp) + bias + mask_s[...])
    carry_s[...] = carry

    m_prev = m_s[...]
    m_new = m_prev
    for s in logits:
        m_new = jnp.maximum(m_new, jnp.max(s, axis=-1, keepdims=True))
    alpha = jnp.exp2(m_prev - m_new)
    l_new = alpha * l_s[...]
    acc = alpha * acc_s[...]
    for u in range(pps):
        p = jnp.exp2(logits[u] - m_new)
        l_new = l_new + jnp.sum(p, axis=-1, keepdims=True)
        vp = v_refs[u][...].reshape(PAGE_ROWS, HEAD_DIM).astype(BF16)
        acc = acc + _dot(p.astype(BF16), vp)
    m_s[...] = m_new
    l_s[...] = l_new
    acc_s[...] = acc

    @pl.when(pg == pl.num_programs(1) - 1)
    def _():
        o_ref[...] = acc / l_new


def _decode_attention(page_table, q, k_new, v_new, lf_new, cache_k, cache_v, cache_lf, layer):
    nb, n_pages = page_table.shape
    pps = PAGES_PER_STEP
    assert n_pages % pps == 0

    def new_map(b, pg, pt):
        return (b, 0, 0)

    def page_spec(block, u):
        zeros = (0,) * (len(block) - 2)
        return pl.BlockSpec(
            block, lambda b, pg, pt: (layer, pt[b, n_pages - 1 - (pg * pps + u)]) + zeros)

    kv_block = (None, None, PAGE_SIZE, N_HEADS, HEAD_DIM)
    lf_block = (None, None, LF_ROWS, LANES)
    col = pltpu.VMEM((DEC_ROWS, 1), F32)
    grid_spec = pltpu.PrefetchScalarGridSpec(
        num_scalar_prefetch=1,
        grid=(nb, n_pages // pps),
        in_specs=[
            pl.BlockSpec((None, DEC_ROWS, HEAD_DIM), new_map),
            pl.BlockSpec((None, LANES, HEAD_DIM), new_map),
            pl.BlockSpec((None, LANES, HEAD_DIM), new_map),
            pl.BlockSpec((None, 1, LANES), new_map),
            *[page_spec(kv_block, u) for u in range(pps)],
            *[page_spec(kv_block, u) for u in range(pps)],
            *[page_spec(lf_block, u) for u in range(pps)],
        ],
        out_specs=pl.BlockSpec((None, DEC_ROWS, HEAD_DIM), new_map),
        scratch_shapes=[col, col, pltpu.VMEM((DEC_ROWS, HEAD_DIM), F32),
                        pltpu.VMEM((1, LANES), F32),
                        pltpu.VMEM((DEC_ROWS, PAGE_ROWS), F32)],
    )
    return pl.pallas_call(
        functools.partial(_decode_kernel, pps=pps),
        grid_spec=grid_spec,
        out_shape=jax.ShapeDtypeStruct((nb, DEC_ROWS, HEAD_DIM), F32),
        compiler_params=_params(2),
        name="fox_sample_attention",
    )(page_table, q, k_new, v_new, lf_new,
      *([cache_k] * pps), *([cache_v] * pps), *([cache_lf] * pps))


OUT_TILE = 512


def _oproj_kernel(o_ref, w_ref, h_ref, out_ref, *, head_major):
    if head_major:
        o = jnp.concatenate([o_ref[h] for h in range(N_HEADS)], axis=-1)
    else:
        o = o_ref[...].astype(BF16)
    out_ref[...] = h_ref[...] + _dot(o, w_ref[...])


def _oproj(o, w, h, *, fox_idx, tm, head_major):
    m = h.shape[0]
    if head_major:
        o_spec = pl.BlockSpec((N_HEADS, tm, HEAD_DIM), lambda i, j: (0, i, 0))
    else:
        o_spec = pl.BlockSpec((tm, D_MODEL), lambda i, j: (i, 0))
    return pl.pallas_call(
        functools.partial(_oproj_kernel, head_major=head_major),
        grid=(m // tm, D_MODEL // OUT_TILE),
        in_specs=[o_spec,
                  pl.BlockSpec((None, D_MODEL, OUT_TILE), lambda i, j: (fox_idx, 0, j)),
                  pl.BlockSpec((tm, OUT_TILE), lambda i, j: (i, j))],
        out_specs=pl.BlockSpec((tm, OUT_TILE), lambda i, j: (i, j)),
        out_shape=jax.ShapeDtypeStruct((m, D_MODEL), F32),
        compiler_params=_params(2),
        name="fox_out_proj",
    )(o, w, h)


N_OUT_TILES = D_MODEL // OUT_TILE


def _ffn_ple_kernel(h_ref, g_ref, wu_ref, wv_ref, cw_ref, cb_ref, wd_ref, prev_ref,
                    gp_ref, p_ref, wpe_ref, wpg_ref, gf_ref,
                    out_ref, tail_ref, xn_s, p_s, ue_s, carry_s,
                    *, tm, shift, keep, tiles_per_seq, nj, final_norm):
    i = pl.program_id(0)
    j = pl.program_id(1)

    @pl.when(j == 0)
    def _():
        xn_s[...] = _rms(h_ref[...], g_ref[...]).astype(BF16)
        out_ref[...] = h_ref[...]

    @pl.when(j < nj)
    def _():
        @pl.when(i % tiles_per_seq == 0)
        def _():
            carry_s[j] = prev_ref[...]

        xn = xn_s[...]
        u = _dot(xn, wu_ref[...])
        v = _dot(xn, wv_ref[...])
        ue_s[0:keep, :] = carry_s[j]
        ue_s[keep:keep + tm, :] = u
        cw = cw_ref[...]
        c = cb_ref[...]
        c = c + cw[0:1, :] * ue_s[keep - 2 * shift:keep - 2 * shift + tm, :]
        c = c + cw[1:2, :] * ue_s[keep - shift:keep - shift + tm, :]
        c = c + cw[2:3, :] * u
        gelu = 0.5 * c * (1.0 + jnp.tanh(0.7978845608028654 * (c + 0.044715 * (c * c * c))))
        out_ref[...] += _dot((gelu * v).astype(BF16), wd_ref[...])
        tail = ue_s[tm:tm + keep, :]
        carry_s[j] = tail
        tail_ref[...] = tail

    @pl.when(j == nj - 1)
    def _():
        xn_s[...] = _rms(out_ref[...], gp_ref[...]).astype(BF16)
        p_s[...] = p_ref[...].astype(BF16)

    for n in range(N_OUT_TILES):
        @pl.when(j == nj + n)
        def _(n=n):
            cols = slice(n * OUT_TILE, (n + 1) * OUT_TILE)
            e = _dot(p_s[...], wpe_ref[...])
            z = _dot(xn_s[...], wpg_ref[...])
            out_ref[:, cols] = out_ref[:, cols] + e * jax.nn.sigmoid(z)
            if final_norm and n == N_OUT_TILES - 1:
                out_ref[...] = _rms(out_ref[...], gf_ref[...])


def _ffn_ple_layer(h, g, w_uv, conv_w, conv_b, w_d, prev0, g_ple, p, w_pe, w_pg, g_final,
                   *, layer, tm, shift, seq_rows, final_norm):
    m = h.shape[0]
    ff = w_d.shape[1]
    pd = p.shape[2]
    nj = ff // FF_TILE
    keep = prev0.shape[2]
    tps = seq_rows // tm
    assert keep >= CONV_CTX * shift and tm >= keep and keep % SUBLANES == 0
    kern = functools.partial(_ffn_ple_kernel, tm=tm, shift=shift, keep=keep, tiles_per_seq=tps,
                             nj=nj, final_norm=final_norm)

    def ffj(j):
        return jnp.minimum(j, nj - 1)

    def plej(j):
        return jnp.clip(j - nj, 0, N_OUT_TILES - 1)

    row = pl.BlockSpec((None, 1, D_MODEL), lambda i, j: (layer, 0, 0))
    return pl.pallas_call(
        kern,
        grid=(m // tm, nj + N_OUT_TILES),
        in_specs=[
            pl.BlockSpec((tm, D_MODEL), lambda i, j: (i, 0)),
            row,
            pl.BlockSpec((None, D_MODEL, FF_TILE), lambda i, j: (layer, 0, ffj(j))),
            pl.BlockSpec((None, D_MODEL, FF_TILE), lambda i, j: (layer, 0, nj + ffj(j))),
            pl.BlockSpec((None, CONV_WIDTH, FF_TILE), lambda i, j: (layer, 0, ffj(j))),
            pl.BlockSpec((None, 1, FF_TILE), lambda i, j: (layer, 0, ffj(j))),
            pl.BlockSpec((None, FF_TILE, D_MODEL), lambda i, j: (layer, ffj(j), 0)),
            pl.BlockSpec((None, None, keep, FF_TILE), lambda i, j: (layer, i // tps, 0, ffj(j))),
            row,
            pl.BlockSpec((None, tm, pd), lambda i, j: (layer, i, 0)),
            pl.BlockSpec((None, pd, OUT_TILE), lambda i, j: (layer, 0, plej(j))),
            pl.BlockSpec((None, D_MODEL, OUT_TILE), lambda i, j: (layer, 0, plej(j))),
            pl.BlockSpec((1, D_MODEL), lambda i, j: (0, 0)),
        ],
        out_specs=[
            pl.BlockSpec((tm, D_MODEL), lambda i, j: (i, 0)),
            pl.BlockSpec((None, keep, FF_TILE), lambda i, j: (i, 0, ffj(j))),
        ],
        out_shape=[jax.ShapeDtypeStruct((m, D_MODEL), F32),
                   jax.ShapeDtypeStruct((m // tm, keep, ff), F32)],
        scratch_shapes=[pltpu.VMEM((tm, D_MODEL), BF16),
                        pltpu.VMEM((tm, pd), BF16),
                        pltpu.VMEM((keep + tm, FF_TILE), F32),
                        pltpu.VMEM((nj, keep, FF_TILE), F32)],
        compiler_params=_params(2),
        name="conv_ffn_ple",
    )(h, g, w_uv, w_uv, conv_w, conv_b, w_d, prev0, g_ple, p, w_pe, w_pg, g_final)


CAST_ROWS = 256


def _cast_up_kernel(w_ref, o_ref, *, d_ff, ff_pad):
    o_ref[:, 0:d_ff] = w_ref[:, 0:d_ff].astype(BF16)
    o_ref[:, d_ff:ff_pad] = jnp.zeros((o_ref.shape[0], ff_pad - d_ff), BF16)
    o_ref[:, ff_pad:ff_pad + d_ff] = w_ref[:, d_ff:2 * d_ff].astype(BF16)
    o_ref[:, ff_pad + d_ff:2 * ff_pad] = jnp.zeros((o_ref.shape[0], ff_pad - d_ff), BF16)


def _cast_up(w_up, ff_pad):
    depth, d, two_ff = w_up.shape
    d_ff = two_ff // 2
    return pl.pallas_call(
        functools.partial(_cast_up_kernel, d_ff=d_ff, ff_pad=ff_pad),
        grid=(depth, d // CAST_ROWS),
        in_specs=[pl.BlockSpec((None, CAST_ROWS, two_ff), lambda l, r: (l, r, 0))],
        out_specs=pl.BlockSpec((None, CAST_ROWS, 2 * ff_pad), lambda l, r: (l, r, 0)),
        out_shape=jax.ShapeDtypeStruct((depth, d, 2 * ff_pad), BF16),
        compiler_params=_params(2),
        name="cast_w_up",
    )(w_up)


def _cast_down_kernel(w_ref, o_ref, *, d_ff):
    o_ref[0:d_ff, :] = w_ref[...].astype(BF16)
    o_ref[d_ff:, :] = jnp.zeros((o_ref.shape[0] - d_ff, o_ref.shape[1]), BF16)


def _cast_down(w_down, ff_pad):
    depth, d_ff, d = w_down.shape
    return pl.pallas_call(
        functools.partial(_cast_down_kernel, d_ff=d_ff),
        grid=(depth, d // OUT_TILE),
        in_specs=[pl.BlockSpec((None, d_ff, OUT_TILE), lambda l, c: (l, 0, c))],
        out_specs=pl.BlockSpec((None, ff_pad, OUT_TILE), lambda l, c: (l, 0, c)),
        out_shape=jax.ShapeDtypeStruct((depth, ff_pad, d), BF16),
        compiler_params=_params(2),
        name="cast_w_down",
    )(w_down)


PROMPT_TILE = 512
QKV_ROWS = 1024
FLASH_TILE = 512


def _trunk(h, p, wts, *, batch, steps, time_major, pos0, pool_prev, conv_prev, decode):
    depth = p.shape[0]
    m = h.shape[0]
    if time_major:
        tm, tm_qkv, shift, seq_rows = m, m, batch, m
    else:
        tm, tm_qkv, shift, seq_rows = PROMPT_TILE, QKV_ROWS, 1, steps
    pool_tails, conv_tails, lfs = [], [], []
    k_all = jnp.zeros((depth // 2, m, D_MODEL), F32)
    v_all = jnp.ones((depth // 2, m, D_MODEL), F32)
    for i in range(depth):
        j = i // 2
        if i % 2 == 0:
            h, tail = _pool_layer(h, wts["g_mix"], pool_prev, wts["w_pool"], wts["pool_scale"],
                                  layer=i, pool_idx=j, tm=tm, shift=shift,
                                  seq_rows=seq_rows, pos0=pos0)
            pool_tails.append(tail)
        else:
            qh, kh, vh, k_all, v_all, lf = _qkvf(
                h, wts["g_mix"], wts["w_qkvf"], wts["w_f"], wts["b_f"], k_all, v_all,
                layer=i, fox_idx=j, tm=tm_qkv, transpose_qv=decode is None)
            lfs.append(lf)
            if decode is None:
                lf_t = lf.reshape(batch, steps, N_HEADS).transpose(0, 2, 1)
                ct = _cumsum_last(lf_t)
                c = ct.transpose(0, 2, 1).reshape(m, N_HEADS)
                q_aug, k_aug = _augment(qh, kh, c, ct, tm=FLASH_TILE)
                o = _flash_prompt(q_aug, k_aug, vh, batch=batch, seq=steps, tq=FLASH_TILE)
                h = _oproj(o, wts["w_o"], h, fox_idx=j, tm=tm, head_major=True)
            else:
                page_table, cache_k, cache_v, cache_lf = decode

                def per_seq(a):
                    a = a.reshape(steps, batch, N_HEADS, -1).transpose(1, 0, 2, 3)
                    return a.reshape(batch, steps * N_HEADS, -1)

                def as_page(a):
                    return jnp.pad(a, ((0, 0), (0, LANES - DEC_ROWS), (0, 0))).astype(BF16)

                q_rows = per_seq(qh.transpose(1, 0, 2).reshape(m, D_MODEL))
                lf_rows = per_seq(lf).reshape(batch, 1, DEC_ROWS)
                lf_rows = jnp.pad(lf_rows, ((0, 0), (0, 0), (0, LANES - DEC_ROWS)))
                o = _decode_attention(page_table, q_rows, as_page(per_seq(k_all[j])),
                                      as_page(per_seq(v_all[j])), lf_rows,
                                      cache_k, cache_v, cache_lf, j)
                o = o.reshape(batch, steps, D_MODEL).transpose(1, 0, 2).reshape(m, D_MODEL)
                h = _oproj(o, wts["w_o"], h, fox_idx=j, tm=tm, head_major=False)
        h, ctail = _ffn_ple_layer(
            h, wts["g_ffn"], wts["w_uv"], wts["conv_w"], wts["conv_b"], wts["w_d"], conv_prev,
            wts["g_ple"], p, wts["w_pe"], wts["w_pg"], wts["g_final"],
            layer=i, tm=tm, shift=shift, seq_rows=seq_rows, final_norm=(i == depth - 1))
        conv_tails.append(ctail)
    return h, pool_tails, conv_tails, k_all, v_all, lfs


def kernel(x_prompt, x_sample, state_pool, state_conv, cache_k, cache_v, cache_lf, page_table,
           p_prompt, p_sample, g_mix, w_pool, pool_scale, w_qkvf, b_f, w_o,
           g_ffn, w_up, conv_w, conv_b, w_down, w_pe, g_ple, w_pg, g_final):
    depth = g_mix.shape[0]
    bp, sp, _ = x_prompt.shape
    bs, ts, _ = x_sample.shape
    d_ff = w_down.shape[1]
    ff_pad = pl.cdiv(d_ff, FF_TILE) * FF_TILE
    padc = ff_pad - d_ff
    n_pool = w_pool.shape[0]
    n_fox = w_o.shape[0]
    assert ts == DEC_T

    w_qkvf_b = w_qkvf.astype(BF16)
    wts = {
        "g_mix": g_mix[:, None, :], "g_ffn": g_ffn[:, None, :], "g_ple": g_ple[:, None, :],
        "g_final": g_final[None, :],
        "w_pool": w_pool.astype(BF16), "pool_scale": pool_scale[:, None, :],
        "w_qkvf": w_qkvf_b,
        "w_f": w_qkvf_b[:, :, 3 * D_MODEL:],
        "b_f": b_f[:, None, :],
        "w_o": w_o.astype(BF16),
        "w_uv": _cast_up(w_up, ff_pad),
        "conv_w": jnp.pad(conv_w, ((0, 0), (0, 0), (0, padc))),
        "conv_b": jnp.pad(conv_b, ((0, 0), (0, padc)))[:, None, :],
        "w_d": _cast_down(w_down, ff_pad),
        "w_pe": w_pe.astype(BF16), "w_pg": w_pg.astype(BF16),
    }

    pool_prev_p = jnp.zeros((n_pool, bp, POOL_SPAN, D_MODEL), F32)
    conv_prev_p = jnp.zeros((depth, bp, SUBLANES, ff_pad), F32)
    y_p, pool_p, conv_p, k_p, v_p, lf_p = _trunk(
        x_prompt.reshape(bp * sp, D_MODEL), p_prompt.reshape(depth, bp * sp, -1), wts,
        batch=bp, steps=sp, time_major=False, pos0=0,
        pool_prev=pool_prev_p, conv_prev=conv_prev_p, decode=None)

    n_pages = page_table.shape[1]
    pool_prev_s = jnp.pad(state_pool.transpose(0, 2, 1, 3), ((0, 0), (1, 0), (0, 0), (0, 0)))
    pool_prev_s = pool_prev_s.reshape(n_pool, 1, POOL_SPAN * bs, D_MODEL)
    conv_prev_s = jnp.pad(state_conv.transpose(0, 2, 1, 3), ((0, 0), (0, 0), (0, 0), (0, padc)))
    conv_prev_s = conv_prev_s.reshape(depth, 1, CONV_CTX * bs, ff_pad)
    n_phys = cache_k.shape[1]
    decode = (page_table, cache_k, cache_v,
              cache_lf.reshape(n_fox, n_phys, LF_ROWS, LANES))
    y_s, pool_s, conv_s, k_s, v_s, lf_s = _trunk(
        x_sample.transpose(1, 0, 2).reshape(ts * bs, D_MODEL),
        p_sample.transpose(0, 2, 1, 3).reshape(depth, ts * bs, -1), wts,
        batch=bs, steps=ts, time_major=True, pos0=n_pages * PAGE_SIZE,
        pool_prev=pool_prev_s, conv_prev=conv_prev_s, decode=decode)

    def from_tm(a, lead):
        return a.reshape(lead, bs, -1).transpose(1, 0, 2)

    ctx = POOL_SPAN - 1
    out_pool_p = jnp.stack([t[:, 1:, :] for t in pool_p])
    out_pool_s = jnp.stack([from_tm(t[0], POOL_SPAN)[:, 1:, :] for t in pool_s])
    tps = sp // PROMPT_TILE
    out_conv_p = jnp.stack([t[tps - 1::tps, SUBLANES - CONV_CTX:, :d_ff] for t in conv_p])
    out_conv_s = jnp.stack([from_tm(t[0], CONV_CTX)[:, :, :d_ff] for t in conv_s])
    assert out_pool_p.shape[2] == ctx

    def heads_p(a):
        return a.reshape(n_fox, bp, sp, N_HEADS, HEAD_DIM)

    def heads_s(a):
        return a.reshape(n_fox, ts, bs, N_HEADS, HEAD_DIM).transpose(0, 2, 1, 3, 4)

    return (y_p.reshape(bp, sp, D_MODEL),
            from_tm(y_s, ts),
            out_pool_p, out_pool_s, out_conv_p, out_conv_s,
            heads_p(k_p), heads_p(v_p),
            jnp.stack([a.reshape(bp, sp, N_HEADS) for a in lf_p]),
            heads_s(k_s), heads_s(v_s),
            jnp.stack([from_tm(a, ts) for a in lf_s]))
```
